```python
import jax, jax.numpy as jnp
from jax import lax
import numpy as np

D_MODEL = 1024
BATCH = 1
SEQ = 16384
DEPTH = 4

CHUNK = 128
SGU_GROUPS = 8
SGU_WIDTH = D_MODEL
SGU_GROUP_DIM = SGU_WIDTH // SGU_GROUPS
SB_HEADS = 8
SB_HEAD_DIM = 128
SB_WIDTH = SB_HEADS * SB_HEAD_DIM
Q_BLOCK = 128
D_FF = 4 * D_MODEL
CONV_WIDTH = 3
N_MOD = 6
EPS = 1e-6
IN_COLS = 2 * SGU_WIDTH + 3 * SB_WIDTH + 2 * D_MODEL

kernel_name = "hybrid_sgu_stickbreaking_convffn_trunk"


def _rmsnorm(x, g):
    xf = x.astype(jnp.float32)
    y = xf * lax.rsqrt(jnp.mean(xf * xf, axis=-1, keepdims=True) + EPS)
    return (y * g.astype(jnp.float32)).astype(x.dtype)


def _layernorm(x, g):
    xf = x.astype(jnp.float32)
    mu = jnp.mean(xf, axis=-1, keepdims=True)
    var = jnp.mean(jnp.square(xf - mu), axis=-1, keepdims=True)
    return ((xf - mu) * lax.rsqrt(var + EPS) * g.astype(jnp.float32)).astype(x.dtype)


def _modulate(h, shift, scale):
    return h * (1.0 + scale[:, None, :]) + shift[:, None, :]


def _chunked_sgu(u, v, g_sgu, w_spatial, b_spatial):
    b, s, _ = v.shape
    vn = _layernorm(v, g_sgu)
    vc = vn.reshape(b, s // CHUNK, CHUNK, SGU_GROUPS, SGU_GROUP_DIM)
    causal = jnp.tril(jnp.ones((CHUNK, CHUNK), dtype=bool))
    w_masked = jnp.where(causal[None], w_spatial, 0.0).astype(v.dtype)
    f = jnp.einsum('gts,bcsgd->bctgd', w_masked, vc)
    f = f + jnp.transpose(b_spatial, (1, 0)).astype(v.dtype)[None, None, :, :, None]
    return u * f.reshape(b, s, SGU_WIDTH)


def _stick_breaking_attention(q, k, v):
    b, h, s, dh = q.shape
    nb = s // Q_BLOCK
    scale = SB_HEAD_DIM ** -0.5
    qf = q.astype(jnp.float32) * scale
    kf = k.astype(jnp.float32)
    vf = v.astype(jnp.float32)
    outs = []
    for i in range(nb):
        kv_len = (i + 1) * Q_BLOCK
        qb = qf[:, :, i * Q_BLOCK:kv_len]
        z = jnp.einsum('bhqd,bhkd->bhqk', qb, kf[:, :, :kv_len])
        q_pos = i * Q_BLOCK + jnp.arange(Q_BLOCK)
        mask = jnp.arange(kv_len)[None, :] < q_pos[:, None]
        log_1m_beta = jnp.where(mask, jax.nn.log_sigmoid(-z), 0.0)
        suffix = lax.cumsum(log_1m_beta, axis=3, reverse=True)
        attn = jnp.exp(jnp.where(mask, z + suffix, -jnp.inf))
        outs.append(jnp.einsum('bhqk,bhkd->bhqd', attn, vf[:, :, :kv_len]))
    out = jnp.concatenate(outs, axis=2)
    out = jnp.transpose(out, (0, 2, 1, 3)).reshape(b, s, h * dh)
    return out.astype(v.dtype)


def _token_mixer(h, w_in, g_sgu, w_spatial, b_spatial, w_a_out, w_b_out, w_o):
    b, s, _ = h.shape
    proj = h @ w_in
    split_at = np.cumsum([SGU_WIDTH, SGU_WIDTH, SB_WIDTH, SB_WIDTH, SB_WIDTH, D_MODEL])
    u_a, v_a, q_b, k_b, v_b, gate_a, gate_b = jnp.split(proj, split_at, axis=-1)
    y_a = _chunked_sgu(jax.nn.gelu(u_a, approximate=False), jax.nn.gelu(v_a, approximate=False),
                       g_sgu, w_spatial, b_spatial) @ w_a_out
    to_heads = lambda t: jnp.transpose(t.reshape(b, s, SB_HEADS, SB_HEAD_DIM), (0, 2, 1, 3))
    y_b = _stick_breaking_attention(to_heads(q_b), to_heads(k_b), to_heads(v_b)) @ w_b_out
    merged = jax.nn.sigmoid(gate_a) * y_a + jax.nn.sigmoid(gate_b) * y_b
    return merged @ w_o


def _conv_ffn(h, w_up, conv_w, conv_b, w_down):
    s = h.shape[1]
    up = h @ w_up
    up_p = jnp.pad(up, ((0, 0), (CONV_WIDTH - 1, 0), (0, 0)))
    conv = conv_b[None, None, :]
    for kk in range(CONV_WIDTH):
        conv = conv + conv_w[kk][None, None, :] * up_p[:, kk:kk + s, :]
    a, bval = jnp.split(conv, 2, axis=-1)
    return (jax.nn.gelu(a, approximate=True) * bval) @ w_down


def setup_inputs(seed: int = 0) -> dict:
    key = jax.random.key(seed)
    ks = jax.random.split(key, 20)
    L, D = DEPTH, D_MODEL
    nrm = lambda k, shape, sc: jax.random.normal(k, shape, jnp.float32) * sc
    return {
        "x": nrm(ks[0], (BATCH, SEQ, D), 1.0),
        "c": nrm(ks[1], (BATCH, D), 1.0),
        "w_ada": nrm(ks[2], (L, D, N_MOD * D), 0.1 * D ** -0.5),
        "b_ada": nrm(ks[3], (L, N_MOD * D), 0.05),
        "g_pre_mix": 1.0 + nrm(ks[4], (L, D), 0.05),
        "g_post_mix": 1.0 + nrm(ks[5], (L, D), 0.05),
        "g_pre_ffn": 1.0 + nrm(ks[6], (L, D), 0.05),
        "g_post_ffn": 1.0 + nrm(ks[7], (L, D), 0.05),
        "w_in": nrm(ks[8], (L, D, IN_COLS), D ** -0.5),
        "g_sgu": 1.0 + nrm(ks[9], (L, SGU_WIDTH), 0.05),
        "w_spatial": nrm(ks[10], (L, SGU_GROUPS, CHUNK, CHUNK), CHUNK ** -0.5),
        "b_spatial": 1.0 + nrm(ks[11], (L, SGU_GROUPS, CHUNK), 0.1),
        "w_a_out": nrm(ks[12], (L, SGU_WIDTH, D), SGU_WIDTH ** -0.5),
        "w_b_out": nrm(ks[13], (L, SB_WIDTH, D), SB_WIDTH ** -0.5),
        "w_o": nrm(ks[14], (L, D, D), D ** -0.5),
        "w_up": nrm(ks[15], (L, D, 2 * D_FF), D ** -0.5),
        "conv_w": nrm(ks[16], (L, CONV_WIDTH, 2 * D_FF), CONV_WIDTH ** -0.5),
        "conv_b": nrm(ks[17], (L, 2 * D_FF), 0.02),
        "w_down": nrm(ks[18], (L, D_FF, D), D_FF ** -0.5),
    }


def reference(x, c, w_ada, b_ada, g_pre_mix, g_post_mix, g_pre_ffn, g_post_ffn,
              w_in, g_sgu, w_spatial, b_spatial, w_a_out, w_b_out, w_o,
              w_up, conv_w, conv_b, w_down):
    c_act = jax.nn.silu(c)
    for l in range(DEPTH):
        mod = c_act @ w_ada[l] + b_ada[l]
        shift_m, scale_m, gate_m, shift_f, scale_f, gate_f = jnp.split(mod, N_MOD, axis=-1)
        h = _modulate(_rmsnorm(x, g_pre_mix[l]), shift_m, scale_m)
        y = _token_mixer(h, w_in[l], g_sgu[l], w_spatial[l], b_spatial[l],
                         w_a_out[l], w_b_out[l], w_o[l])
        x = x + (1.0 + gate_m)[:, None, :] * _rmsnorm(y, g_post_mix[l])
        h = _modulate(_rmsnorm(x, g_pre_ffn[l]), shift_f, scale_f)
        y = _conv_ffn(h, w_up[l], conv_w[l], conv_b[l], w_down[l])
        x = x + (1.0 + gate_f)[:, None, :] * _rmsnorm(y, g_post_ffn[l])
    return x
```

```python
import functools
import math

import jax
import jax.numpy as jnp
from jax import lax
from jax.experimental import pallas as pl
from jax.experimental.pallas import tpu as pltpu

F32 = jnp.float32
BF16 = jnp.bfloat16

D_MODEL = 1024
N_MOD = 6
CHUNK = 128
GROUP_DIM = 128
N_GROUPS = 8
HEAD_DIM = 128
N_HEADS = 8
D_FF = 4 * D_MODEL
CONV_WIDTH = 3
EPS = 1e-6
N_SEG = 7

V7X_VMEM_BYTES = 64 * 1024 * 1024
V7X_SUBLANES = 8
VMEM_LIMIT = V7X_VMEM_BYTES - 8 * 1024 * 1024

ROW_TILE = 512
FF_TILE = 1024
Q_TILE = 256
K_TILE = 256
V7X_BF16_ROWS_PER_VREG = 2 * V7X_SUBLANES
HALO = V7X_BF16_ROWS_PER_VREG
LOG_F32_UNDERFLOW = -104.0


def _rms(x):
    return x * lax.rsqrt(jnp.mean(x * x, axis=-1, keepdims=True) + EPS)


def _gelu_erf(x):
    return 0.5 * x * (1.0 + lax.erf(x * (1.0 / math.sqrt(2.0))))


def _gelu_tanh(x):
    return 0.5 * x * (1.0 + jnp.tanh(math.sqrt(2.0 / math.pi) * (x + 0.044715 * (x * x * x))))


def _sigmoid(x):
    return 1.0 / (1.0 + jnp.exp(-x))


def _adaln_kernel(c_ref, w_ref, b_ref, o_ref):
    c = c_ref[...]
    c_act = c * _sigmoid(c)
    o_ref[...] = jnp.sum(c_act * w_ref[...], axis=0, keepdims=True) + b_ref[...]


def _adaln_mod(c, w_ada, b_ada):
    depth, d, n = w_ada.shape
    tn = 1536
    return pl.pallas_call(
        _adaln_kernel,
        grid=(depth, n // tn),
        in_specs=[
            pl.BlockSpec((d, 1), lambda l, j: (0, 0)),
            pl.BlockSpec((None, d, tn), lambda l, j: (l, 0, j)),
            pl.BlockSpec((None, 1, tn), lambda l, j: (l, 0, j)),
        ],
        out_specs=pl.BlockSpec((None, 1, tn), lambda l, j: (l, 0, j)),
        out_shape=jax.ShapeDtypeStruct((depth, 1, n), F32),
        compiler_params=pltpu.CompilerParams(
            dimension_semantics=("arbitrary", "arbitrary"), vmem_limit_bytes=VMEM_LIMIT),
        name="adaln_mod",
    )(c.reshape(d, 1), w_ada, b_ada.reshape(depth, 1, n))


def _mixer_in_kernel(x_ref, mod_ref, gpre_ref, w_ref, gsgu_ref, wsp_ref, bsp_ref, wa_ref,
                     q_ref, k_ref, v_ref, yag_ref, gb_ref, h_s, u_s, vn_s, ya_s):
    j = pl.program_id(1)
    tm = x_ref.shape[0]

    @pl.when(j == 0)
    def _():
        hn = _rms(x_ref[...]) * gpre_ref[...]
        h = hn * (1.0 + mod_ref[1:2, :]) + mod_ref[0:1, :]
        h_s[...] = h.astype(BF16)

    def proj():
        return jnp.dot(h_s[...], w_ref[...], preferred_element_type=F32)

    @pl.when(j == 0)
    def _():
        u_s[...] = _gelu_erf(proj())

    @pl.when(j == 1)
    def _():
        vg = _gelu_erf(proj())
        mu = jnp.mean(vg, axis=-1, keepdims=True)
        vc = vg - mu
        var = jnp.mean(vc * vc, axis=-1, keepdims=True)
        vn_s[...] = (vc * lax.rsqrt(var + EPS) * gsgu_ref[...]).astype(BF16)
        causal = (lax.broadcasted_iota(jnp.int32, (CHUNK, CHUNK), 0)
                  >= lax.broadcasted_iota(jnp.int32, (CHUNK, CHUNK), 1))
        for g in range(N_GROUPS):
            cols = slice(g * GROUP_DIM, (g + 1) * GROUP_DIM)
            wg = jnp.where(causal, wsp_ref[g], 0.0).astype(BF16)
            bg = bsp_ref[:, g:g + 1]
            for c in range(tm // CHUNK):
                rows = slice(c * CHUNK, (c + 1) * CHUNK)
                f = jnp.dot(wg, vn_s[rows, cols], preferred_element_type=F32) + bg
                vn_s[rows, cols] = (u_s[rows, cols] * f).astype(BF16)
        ya_s[...] = jnp.dot(vn_s[...], wa_ref[...], preferred_element_type=F32)

    @pl.when(j == 2)
    def _():
        q_ref[...] = (proj() * (HEAD_DIM ** -0.5)).astype(BF16)

    @pl.when(j == 3)
    def _():
        k_ref[...] = proj().astype(BF16)

    @pl.when(j == 4)
    def _():
        v_ref[...] = proj().astype(BF16)

    @pl.when(j == 5)
    def _():
        yag_ref[...] = (_sigmoid(proj()) * ya_s[...]).astype(BF16)

    @pl.when(j == 6)
    def _():
        gb_ref[...] = _sigmoid(proj()).astype(BF16)


def _mixer_in(l, x, mod, g_pre, w_in, g_sgu, w_sp, b_sp_t, w_a):
    s, d = x.shape
    tm = ROW_TILE
    row_spec = pl.BlockSpec((tm, d), lambda i, j: (i, 0))
    vec_spec = pl.BlockSpec((None, 1, d), lambda i, j: (l, 0, 0))
    out = jax.ShapeDtypeStruct((s, d), BF16)
    return pl.pallas_call(
        _mixer_in_kernel,
        grid=(s // tm, N_SEG),
        in_specs=[
            row_spec,
            pl.BlockSpec((None, N_MOD, d), lambda i, j: (l, 0, 0)),
            vec_spec,
            pl.BlockSpec((None, d, d), lambda i, j: (l, 0, j)),
            vec_spec,
            pl.BlockSpec((None, N_GROUPS, CHUNK, CHUNK), lambda i, j: (l, 0, 0, 0)),
            pl.BlockSpec((None, CHUNK, N_GROUPS), lambda i, j: (l, 0, 0)),
            pl.BlockSpec((None, d, d), lambda i, j: (l, 0, 0)),
        ],
        out_specs=[row_spec] * 5,
        out_shape=[out] * 5,
        scratch_shapes=[
            pltpu.VMEM((tm, d), BF16),
            pltpu.VMEM((tm, d), F32),
            pltpu.VMEM((tm, d), BF16),
            pltpu.VMEM((tm, d), F32),
        ],
        compiler_params=pltpu.CompilerParams(
            dimension_semantics=("arbitrary", "arbitrary"), vmem_limit_bytes=VMEM_LIMIT),
        name="mixer_in",
    )(x, mod, g_pre, w_in, g_sgu, w_sp, b_sp_t, w_a)


def _sb_attn_kernel(q_ref, k_ref, v_ref, o_ref, acc_s, carry_s):
    i = pl.program_id(1)
    tq = q_ref.shape[0]
    tk = K_TILE
    q = q_ref[...]
    row = lax.broadcasted_iota(jnp.int32, (tq, tk), 0)
    col = lax.broadcasted_iota(jnp.int32, (tq, tk), 1)
    strictly_causal = col < row
    tri = jnp.where(lax.broadcasted_iota(jnp.int32, (tk, tk), 0)
                    >= lax.broadcasted_iota(jnp.int32, (tk, tk), 1), 1.0, 0.0).astype(BF16)

    def key_block(kb, diagonal):
        start = pl.multiple_of(kb * tk, tk)
        kk = k_ref[pl.ds(start, tk), :]
        vv = v_ref[pl.ds(start, tk), :]
        z = lax.dot_general(q, kk, (((1,), (1,)), ((), ())), preferred_element_type=F32)
        log_1m_beta = -(jnp.maximum(z, 0.0) + jnp.log1p(jnp.exp(-jnp.abs(z))))
        if diagonal:
            log_1m_beta = jnp.where(strictly_causal, log_1m_beta, 0.0)
        hi = log_1m_beta.astype(BF16)
        lo = (log_1m_beta - hi.astype(F32)).astype(BF16)
        suffix = (jnp.dot(hi, tri, preferred_element_type=F32)
                  + jnp.dot(lo, tri, preferred_element_type=F32))
        carry = carry_s[...]
        attn = jnp.exp(z + suffix + carry)
        if diagonal:
            attn = jnp.where(strictly_causal, attn, 0.0)
        acc_s[...] += jnp.dot(attn.astype(BF16), vv, preferred_element_type=F32)
        carry_s[...] = carry + suffix[:, 0:1]

    acc_s[...] = jnp.zeros_like(acc_s)
    carry_s[...] = jnp.zeros_like(carry_s)
    key_block(i, True)

    def cond(state):
        kb, live = state
        return jnp.logical_and(kb >= 0, live)

    def body(state):
        kb, _ = state
        key_block(kb, False)
        return kb - 1, jnp.max(carry_s[...]) > LOG_F32_UNDERFLOW

    lax.while_loop(cond, body, (i - 1, True))
    o_ref[...] = acc_s[...].astype(o_ref.dtype)


def _sb_attn(q, k, v):
    s, d = q.shape
    tq = Q_TILE
    assert Q_TILE == K_TILE
    blk = pl.BlockSpec((tq, HEAD_DIM), lambda h, i: (i, h))
    full = pl.BlockSpec((s, HEAD_DIM), lambda h, i: (0, h))
    return pl.pallas_call(
        _sb_attn_kernel,
        grid=(N_HEADS, s // tq),
        in_specs=[blk, full, full],
        out_specs=blk,
        out_shape=jax.ShapeDtypeStruct((s, d), BF16),
        scratch_shapes=[pltpu.VMEM((tq, HEAD_DIM), F32), pltpu.VMEM((tq, 1), F32)],
        compiler_params=pltpu.CompilerParams(
            dimension_semantics=("arbitrary", "arbitrary"), vmem_limit_bytes=VMEM_LIMIT),
        name="sb_attn",
    )(q, k, v)


def _mix_ffn_kernel(x_ref, attn_ref, yag_ref, gb_ref, mod_ref, gpost_ref, gpre_ref, gpost2_ref,
                    wb_ref, wo_ref, wua_ref, wub_ref, cw_a_ref, cw_b_ref, cb_a_ref, cb_b_ref,
                    wd_ref, o_ref, x1_s, h_s, ua_s, ub_s, acc_s):
    i = pl.program_id(0)
    c = pl.program_id(1)
    tm = x_ref.shape[0]

    @pl.when(jnp.logical_and(c == 0, i == 0))
    def _():
        h_s[0:HALO, :] = jnp.zeros((HALO, h_s.shape[1]), BF16)

    @pl.when(jnp.logical_and(c == 0, i > 0))
    def _():
        h_s[0:HALO, :] = h_s[tm:tm + HALO, :]

    @pl.when(c == 0)
    def _():
        y_b = jnp.dot(attn_ref[...], wb_ref[...], preferred_element_type=F32)
        merged = yag_ref[...].astype(F32) + gb_ref[...].astype(F32) * y_b
        y = jnp.dot(merged.astype(BF16), wo_ref[...], preferred_element_type=F32)
        x1 = x_ref[...] + (1.0 + mod_ref[2:3, :]) * (_rms(y) * gpost_ref[...])
        x1_s[...] = x1
        hn = _rms(x1) * gpre_ref[...]
        h_s[HALO:, :] = (hn * (1.0 + mod_ref[4:5, :]) + mod_ref[3:4, :]).astype(BF16)

    h = h_s[...]
    ua_s[...] = jnp.dot(h, wua_ref[...], preferred_element_type=F32)
    ub_s[...] = jnp.dot(h, wub_ref[...], preferred_element_type=F32)

    def conv(u_s, cw_ref, cb_ref):
        out = cb_ref[...]
        for kk in range(CONV_WIDTH):
            shift = CONV_WIDTH - 1 - kk
            out = out + cw_ref[kk:kk + 1, :] * u_s[HALO - shift:HALO - shift + tm, :]
        return out

    act = _gelu_tanh(conv(ua_s, cw_a_ref, cb_a_ref)) * conv(ub_s, cw_b_ref, cb_b_ref)
    part = jnp.dot(act.astype(BF16), wd_ref[...], preferred_element_type=F32)

    @pl.when(c == 0)
    def _():
        acc_s[...] = part

    @pl.when(c > 0)
    def _():
        acc_s[...] += part

    @pl.when(c == pl.num_programs(1) - 1)
    def _():
        o_ref[...] = x1_s[...] + (1.0 + mod_ref[5:6, :]) * (_rms(acc_s[...]) * gpost2_ref[...])


def _mix_ffn(l, x, attn, yag, gb, mod, g_post, g_pre2, g_post2, w_b, w_o, w_up, conv_w, conv_b,
             w_down):
    s, d = x.shape
    tm, tc = ROW_TILE, FF_TILE
    nc = D_FF // tc
    row_spec = pl.BlockSpec((tm, d), lambda i, c: (i, 0))
    vec_spec = pl.BlockSpec((None, 1, d), lambda i, c: (l, 0, 0))
    sq_spec = pl.BlockSpec((None, d, d), lambda i, c: (l, 0, 0))
    return pl.pallas_call(
        _mix_ffn_kernel,
        grid=(s // tm, nc),
        in_specs=[
            row_spec, row_spec, row_spec, row_spec,
            pl.BlockSpec((None, N_MOD, d), lambda i, c: (l, 0, 0)),
            vec_spec, vec_spec, vec_spec,
            sq_spec, sq_spec,
            pl.BlockSpec((None, d, tc), lambda i, c: (l, 0, c)),
            pl.BlockSpec((None, d, tc), lambda i, c: (l, 0, nc + c)),
            pl.BlockSpec((None, CONV_WIDTH, tc), lambda i, c: (l, 0, c)),
            pl.BlockSpec((None, CONV_WIDTH, tc), lambda i, c: (l, 0, nc + c)),
            pl.BlockSpec((None, 1, tc), lambda i, c: (l, 0, c)),
            pl.BlockSpec((None, 1, tc), lambda i, c: (l, 0, nc + c)),
            pl.BlockSpec((None, tc, d), lambda i, c: (l, c, 0)),
        ],
        out_specs=row_spec,
        out_shape=jax.ShapeDtypeStruct((s, d), F32),
        scratch_shapes=[
            pltpu.VMEM((tm, d), F32),
            pltpu.VMEM((tm + HALO, d), BF16),
            pltpu.VMEM((tm + HALO, tc), F32),
            pltpu.VMEM((tm + HALO, tc), F32),
            pltpu.VMEM((tm, d), F32),
        ],
        compiler_params=pltpu.CompilerParams(
            dimension_semantics=("arbitrary", "arbitrary"), vmem_limit_bytes=VMEM_LIMIT),
        name="mix_ffn",
    )(x, attn, yag, gb, mod, g_post, g_pre2, g_post2, w_b, w_o, w_up, w_up, conv_w, conv_w,
      conv_b, conv_b, w_down)


def kernel(x, c, w_ada, b_ada, g_pre_mix, g_post_mix, g_pre_ffn, g_post_ffn, w_in, g_sgu,
           w_spatial, b_spatial, w_a_out, w_b_out, w_o, w_up, conv_w, conv_b, w_down):
    batch, s, d = x.shape
    depth = w_in.shape[0]
    assert batch == 1 and d == D_MODEL and s % ROW_TILE == 0 and s % Q_TILE == 0
    mod = _adaln_mod(c, w_ada, b_ada).reshape(depth, N_MOD, d)
    vec = lambda g: g.reshape(depth, 1, -1)
    w_in, w_a_out, w_b_out, w_o, w_up, w_down = (
        w.astype(BF16) for w in (w_in, w_a_out, w_b_out, w_o, w_up, w_down))
    b_sp_t = jnp.transpose(b_spatial, (0, 2, 1))
    xs = x.reshape(s, d)
    for l in range(depth):
        q, k, v, yag, gb = _mixer_in(l, xs, mod, vec(g_pre_mix), w_in, vec(g_sgu), w_spatial,
                                     b_sp_t, w_a_out)
        attn = _sb_attn(q, k, v)
        xs = _mix_ffn(l, xs, attn, yag, gb, mod, vec(g_post_mix), vec(g_pre_ffn),
                      vec(g_post_ffn), w_b_out, w_o, w_up, conv_w, vec(conv_b), w_down)
    return xs.reshape(batch, s, d)
```

```python
import math

import jax
import jax.numpy as jnp
from jax import lax
from jax.experimental import pallas as pl
from jax.experimental.pallas import tpu as pltpu

F32 = jnp.float32
BF16 = jnp.bfloat16

D_MODEL = 1024
N_MOD = 6
CHUNK = 128
GROUP_DIM = 128
N_GROUPS = 8
HEAD_DIM = 128
N_HEADS = 8
D_FF = 4 * D_MODEL
CONV_WIDTH = 3
EPS = 1e-6
N_SEG = 7

V7X_VMEM_BYTES = 64 * 1024 * 1024
V7X_SUBLANES = 8
V7X_BF16_ROWS_PER_VREG = 2 * V7X_SUBLANES
VMEM_LIMIT = V7X_VMEM_BYTES - 8 * 1024 * 1024

MIX_ROW_TILE = 1024
FFN_ROW_TILE = 512
FFN_SUB_ROWS = 256
FF_TILE = 1024
Q_TILE = 256
K_TILE = 256
HEADS_PER_STEP = 4
HALO = V7X_BF16_ROWS_PER_VREG
LOG_F32_UNDERFLOW = -104.0


def _rms(x):
    return x * lax.rsqrt(jnp.mean(x * x, axis=-1, keepdims=True) + EPS)


def _gelu_erf(x):
    return 0.5 * x * (1.0 + lax.erf(x * (1.0 / math.sqrt(2.0))))


def _gelu_tanh(x):
    return 0.5 * x * (1.0 + jnp.tanh(math.sqrt(2.0 / math.pi) * (x + 0.044715 * (x * x * x))))


def _sigmoid(x):
    return 1.0 / (1.0 + jnp.exp(-x))


def _adaln_kernel(c_ref, w_ref, b_ref, o_ref):
    c = c_ref[...]
    c_act = c * _sigmoid(c)
    o_ref[...] = jnp.sum(c_act * w_ref[...], axis=0, keepdims=True) + b_ref[...]


def _adaln_mod(c, w_ada, b_ada):
    depth, d, n = w_ada.shape
    tn = 1536
    return pl.pallas_call(
        _adaln_kernel,
        grid=(depth, n // tn),
        in_specs=[
            pl.BlockSpec((d, 1), lambda l, j: (0, 0)),
            pl.BlockSpec((None, d, tn), lambda l, j: (l, 0, j)),
            pl.BlockSpec((None, 1, tn), lambda l, j: (l, 0, j)),
        ],
        out_specs=pl.BlockSpec((None, 1, tn), lambda l, j: (l, 0, j)),
        out_shape=jax.ShapeDtypeStruct((depth, 1, n), F32),
        compiler_params=pltpu.CompilerParams(
            dimension_semantics=("arbitrary", "arbitrary"), vmem_limit_bytes=VMEM_LIMIT),
        name="adaln_mod",
    )(c.reshape(d, 1), w_ada, b_ada.reshape(depth, 1, n))


SEG_GATE_A, SEG_GATE_B, SEG_U, SEG_V, SEG_Q, SEG_K, SEG_VB = range(N_SEG)
FIRST_SEG_COLUMN = 5


def _mixer_in_kernel(x_ref, mod_ref, gpre_ref, w_ref, gsgu_ref, wsp_ref, bsp_ref, wa_ref,
                     q_ref, k_ref, v_ref, yag_ref, gb_ref, h_s, u_s, vn_s):
    j = pl.program_id(1)
    tm = x_ref.shape[0]

    @pl.when(j == 0)
    def _():
        hn = _rms(x_ref[...]) * gpre_ref[...]
        h = hn * (1.0 + mod_ref[1:2, :]) + mod_ref[0:1, :]
        h_s[...] = h.astype(BF16)

    def proj():
        return jnp.dot(h_s[...], w_ref[...], preferred_element_type=F32)

    @pl.when(j == SEG_GATE_A)
    def _():
        yag_ref[...] = _sigmoid(proj()).astype(BF16)

    @pl.when(j == SEG_GATE_B)
    def _():
        gb_ref[...] = _sigmoid(proj()).astype(BF16)

    @pl.when(j == SEG_U)
    def _():
        u_s[...] = _gelu_erf(proj())

    @pl.when(j == SEG_V)
    def _():
        vg = _gelu_erf(proj())
        mu = jnp.mean(vg, axis=-1, keepdims=True)
        vc = vg - mu
        var = jnp.mean(vc * vc, axis=-1, keepdims=True)
        vn_s[...] = (vc * lax.rsqrt(var + EPS) * gsgu_ref[...]).astype(BF16)
        causal = (lax.broadcasted_iota(jnp.int32, (CHUNK, CHUNK), 0)
                  >= lax.broadcasted_iota(jnp.int32, (CHUNK, CHUNK), 1))
        for g in range(N_GROUPS):
            cols = slice(g * GROUP_DIM, (g + 1) * GROUP_DIM)
            wg = jnp.where(causal, wsp_ref[g], 0.0).astype(BF16)
            bg = bsp_ref[:, g:g + 1]
            for c in range(tm // CHUNK):
                rows = slice(c * CHUNK, (c + 1) * CHUNK)
                f = jnp.dot(wg, vn_s[rows, cols], preferred_element_type=F32) + bg
                vn_s[rows, cols] = (u_s[rows, cols] * f).astype(BF16)
        y_a = jnp.dot(vn_s[...], wa_ref[...], preferred_element_type=F32)
        yag_ref[...] = (yag_ref[...].astype(F32) * y_a).astype(BF16)

    @pl.when(j == SEG_Q)
    def _():
        q_ref[...] = (proj() * (HEAD_DIM ** -0.5)).astype(BF16)

    @pl.when(j == SEG_K)
    def _():
        k_ref[...] = proj().astype(BF16)

    @pl.when(j == SEG_VB)
    def _():
        v_ref[...] = proj().astype(BF16)


def _mixer_in(l, x, mod, g_pre, w_in, g_sgu, w_sp, b_sp_t, w_a):
    s, d = x.shape
    tm = MIX_ROW_TILE
    row_spec = pl.BlockSpec((tm, d), lambda i, j: (i, 0))
    vec_spec = pl.BlockSpec((None, 1, d), lambda i, j: (l, 0, 0))
    out = jax.ShapeDtypeStruct((s, d), BF16)
    return pl.pallas_call(
        _mixer_in_kernel,
        grid=(s // tm, N_SEG),
        in_specs=[
            row_spec,
            pl.BlockSpec((None, N_MOD, d), lambda i, j: (l, 0, 0)),
            vec_spec,
            pl.BlockSpec((None, d, d), lambda i, j: (l, 0, (j + FIRST_SEG_COLUMN) % N_SEG)),
            vec_spec,
            pl.BlockSpec((None, N_GROUPS, CHUNK, CHUNK), lambda i, j: (l, 0, 0, 0)),
            pl.BlockSpec((None, CHUNK, N_GROUPS), lambda i, j: (l, 0, 0)),
            pl.BlockSpec((None, d, d), lambda i, j: (l, 0, 0), pipeline_mode=pl.Buffered(1)),
        ],
        out_specs=[row_spec] * 5,
        out_shape=[out] * 5,
        scratch_shapes=[
            pltpu.VMEM((tm, d), BF16),
            pltpu.VMEM((tm, d), F32),
            pltpu.VMEM((tm, d), BF16),
        ],
        compiler_params=pltpu.CompilerParams(
            dimension_semantics=("arbitrary", "arbitrary"), vmem_limit_bytes=VMEM_LIMIT),
        name="mixer_in",
    )(x, mod, g_pre, w_in, g_sgu, w_sp, b_sp_t, w_a)


def _sb_attn_kernel(q_ref, k_ref, v_ref, o_ref, acc_s, carry_s):
    i = pl.program_id(1)
    tq = q_ref.shape[0]
    tk = K_TILE
    row = lax.broadcasted_iota(jnp.int32, (tq, tk), 0)
    col = lax.broadcasted_iota(jnp.int32, (tq, tk), 1)
    strictly_causal = col < row
    tri = jnp.where(lax.broadcasted_iota(jnp.int32, (tk, tk), 0)
                    >= lax.broadcasted_iota(jnp.int32, (tk, tk), 1), 1.0, 0.0).astype(BF16)

    def key_block(kb, diagonal):
        start = pl.multiple_of(kb * tk, tk)
        heads = range(HEADS_PER_STEP)
        lanes = [slice(h * HEAD_DIM, (h + 1) * HEAD_DIM) for h in heads]
        z = [lax.dot_general(q_ref[:, lanes[h]], k_ref[pl.ds(start, tk), lanes[h]],
                             (((1,), (1,)), ((), ())), preferred_element_type=F32)
             for h in heads]
        suffix = []
        for h in heads:
            softplus = jnp.maximum(z[h], 0.0) + jnp.log(1.0 + jnp.exp(-jnp.abs(z[h])))
            if diagonal:
                softplus = jnp.where(strictly_causal, softplus, 0.0)
            hi = softplus.astype(BF16)
            lo = (softplus - hi.astype(F32)).astype(BF16)
            both = jnp.dot(jnp.concatenate([hi, lo], axis=0), tri, preferred_element_type=F32)
            suffix.append(both[:tq] + both[tq:])
        for h in heads:
            carry = carry_s[h]
            attn = jnp.exp(z[h] - suffix[h] - carry)
            if diagonal:
                attn = jnp.where(strictly_causal, attn, 0.0)
            acc_s[:, lanes[h]] += jnp.dot(attn.astype(BF16), v_ref[pl.ds(start, tk), lanes[h]],
                                          preferred_element_type=F32)
            carry_s[h] = carry + suffix[h][:, 0:1]

    acc_s[...] = jnp.zeros_like(acc_s)
    carry_s[...] = jnp.zeros_like(carry_s)
    key_block(i, True)

    def cond(state):
        kb, live = state
        return jnp.logical_and(kb >= 0, live)

    def body(state):
        kb, _ = state
        key_block(kb, False)
        return kb - 1, jnp.min(carry_s[...]) < -LOG_F32_UNDERFLOW

    lax.while_loop(cond, body, (i - 1, True))
    o_ref[...] = acc_s[...].astype(o_ref.dtype)


def _sb_attn(q, k, v):
    s, d = q.shape
    tq = Q_TILE
    assert Q_TILE == K_TILE and N_HEADS % HEADS_PER_STEP == 0
    width = HEADS_PER_STEP * HEAD_DIM
    blk = pl.BlockSpec((tq, width), lambda h, i: (i, h))
    full = pl.BlockSpec((s, width), lambda h, i: (0, h), pipeline_mode=pl.Buffered(1))
    return pl.pallas_call(
        _sb_attn_kernel,
        grid=(N_HEADS // HEADS_PER_STEP, s // tq),
        in_specs=[blk, full, full],
        out_specs=blk,
        out_shape=jax.ShapeDtypeStruct((s, d), BF16),
        scratch_shapes=[pltpu.VMEM((tq, width), F32),
                        pltpu.VMEM((HEADS_PER_STEP, tq, 1), F32)],
        compiler_params=pltpu.CompilerParams(
            dimension_semantics=("arbitrary", "arbitrary"), vmem_limit_bytes=VMEM_LIMIT),
        name="sb_attn",
    )(q, k, v)


def _mix_ffn_kernel(x_ref, attn_ref, yag_ref, gb_ref, mod_ref, gpost_ref, gpre_ref, gpost2_ref,
                    wb_ref, wo_ref, wua_ref, wub_ref, cw_a_ref, cw_b_ref, cb_a_ref, cb_b_ref,
                    wd_ref, o_ref, x1_s, h_s, acc_s):
    i = pl.program_id(0)
    c = pl.program_id(1)
    tm = x_ref.shape[0]

    @pl.when(jnp.logical_and(c == 0, i == 0))
    def _():
        h_s[0:HALO, :] = jnp.zeros((HALO, h_s.shape[1]), BF16)

    @pl.when(jnp.logical_and(c == 0, i > 0))
    def _():
        h_s[0:HALO, :] = h_s[tm:tm + HALO, :]

    @pl.when(c == 0)
    def _():
        y_b = jnp.dot(attn_ref[...], wb_ref[...], preferred_element_type=F32)
        merged = yag_ref[...].astype(F32) + gb_ref[...].astype(F32) * y_b
        y = jnp.dot(merged.astype(BF16), wo_ref[...], preferred_element_type=F32)
        x1 = x_ref[...] + (1.0 + mod_ref[2:3, :]) * (_rms(y) * gpost_ref[...])
        x1_s[...] = x1
        hn = _rms(x1) * gpre_ref[...]
        h_s[HALO:, :] = (hn * (1.0 + mod_ref[4:5, :]) + mod_ref[3:4, :]).astype(BF16)
        acc_s[...] = jnp.zeros_like(acc_s)

    sub = FFN_SUB_ROWS

    def up(r):
        hr = h_s[r * sub:r * sub + sub + HALO, :]
        return (jnp.dot(hr, wua_ref[...], preferred_element_type=F32),
                jnp.dot(hr, wub_ref[...], preferred_element_type=F32))

    def conv(u, cw_ref, cb_ref):
        out = cb_ref[...]
        for kk in range(CONV_WIDTH):
            shift = CONV_WIDTH - 1 - kk
            out = out + cw_ref[kk:kk + 1, :] * u[HALO - shift:HALO - shift + sub, :]
        return out

    def down(r, ua, ub):
        act = _gelu_tanh(conv(ua, cw_a_ref, cb_a_ref)) * conv(ub, cw_b_ref, cb_b_ref)
        acc_s[r * sub:(r + 1) * sub, :] += jnp.dot(act.astype(BF16), wd_ref[...],
                                                   preferred_element_type=F32)

    ups = up(0)
    for r in range(tm // sub):
        nxt = up(r + 1) if r + 1 < tm // sub else None
        down(r, *ups)
        ups = nxt

    @pl.when(c == pl.num_programs(1) - 1)
    def _():
        o_ref[...] = x1_s[...] + (1.0 + mod_ref[5:6, :]) * (_rms(acc_s[...]) * gpost2_ref[...])


def _mix_ffn(l, x, attn, yag, gb, mod, g_post, g_pre2, g_post2, w_b, w_o, w_up, conv_w, conv_b,
             w_down):
    s, d = x.shape
    tm, tc = FFN_ROW_TILE, FF_TILE
    nc = D_FF // tc
    row_spec = pl.BlockSpec((tm, d), lambda i, c: (i, 0))
    vec_spec = pl.BlockSpec((None, 1, d), lambda i, c: (l, 0, 0))
    sq_spec = pl.BlockSpec((None, d, d), lambda i, c: (l, 0, 0), pipeline_mode=pl.Buffered(1))
    return pl.pallas_call(
        _mix_ffn_kernel,
        grid=(s // tm, nc),
        in_specs=[
            row_spec, row_spec, row_spec, row_spec,
            pl.BlockSpec((None, N_MOD, d), lambda i, c: (l, 0, 0)),
            vec_spec, vec_spec, vec_spec,
            sq_spec, sq_spec,
            pl.BlockSpec((None, d, tc), lambda i, c: (l, 0, c)),
            pl.BlockSpec((None, d, tc), lambda i, c: (l, 0, nc + c)),
            pl.BlockSpec((None, CONV_WIDTH, tc), lambda i, c: (l, 0, c)),
            pl.BlockSpec((None, CONV_WIDTH, tc), lambda i, c: (l, 0, nc + c)),
            pl.BlockSpec((None, 1, tc), lambda i, c: (l, 0, c)),
            pl.BlockSpec((None, 1, tc), lambda i, c: (l, 0, nc + c)),
            pl.BlockSpec((None, tc, d), lambda i, c: (l, c, 0)),
        ],
        out_specs=row_spec,
        out_shape=jax.ShapeDtypeStruct((s, d), F32),
        scratch_shapes=[
            pltpu.VMEM((tm, d), F32),
            pltpu.VMEM((tm + HALO, d), BF16),
            pltpu.VMEM((tm, d), F32),
        ],
        compiler_params=pltpu.CompilerParams(
            dimension_semantics=("arbitrary", "arbitrary"), vmem_limit_bytes=VMEM_LIMIT),
        name="mix_ffn",
    )(x, attn, yag, gb, mod, g_post, g_pre2, g_post2, w_b, w_o, w_up, w_up, conv_w, conv_w,
      conv_b, conv_b, w_down)


def kernel(x, c, w_ada, b_ada, g_pre_mix, g_post_mix, g_pre_ffn, g_post_ffn, w_in, g_sgu,
           w_spatial, b_spatial, w_a_out, w_b_out, w_o, w_up, conv_w, conv_b, w_down):
    batch, s, d = x.shape
    depth = w_in.shape[0]
    assert batch == 1 and d == D_MODEL
    assert s % MIX_ROW_TILE == 0 and s % FFN_ROW_TILE == 0 and s % Q_TILE == 0
    mod = _adaln_mod(c, w_ada, b_ada).reshape(depth, N_MOD, d)
    vec = lambda g: g.reshape(depth, 1, -1)
    w_in, w_a_out, w_b_out, w_o, w_up, w_down = (
        w.astype(BF16) for w in (w_in, w_a_out, w_b_out, w_o, w_up, w_down))
    b_sp_t = jnp.transpose(b_spatial, (0, 2, 1))
    xs = x.reshape(s, d)
    for l in range(depth):
        q, k, v, yag, gb = _mixer_in(l, xs, mod, vec(g_pre_mix), w_in, vec(g_sgu), w_spatial,
                                     b_sp_t, w_a_out)
        attn = _sb_attn(q, k, v)
        xs = _mix_ffn(l, xs, attn, yag, gb, mod, vec(g_post_mix), vec(g_pre_ffn),
                      vec(g_post_ffn), w_b_out, w_o, w_up, conv_w, vec(conv_b), w_down)
    return xs.reshape(batch, s, d)
```

```python
import math

import jax
import jax.numpy as jnp
from jax import lax
from jax.experimental import pallas as pl
from jax.experimental.pallas import tpu as pltpu

F32 = jnp.float32
BF16 = jnp.bfloat16

D_MODEL = 1024
N_MOD = 6
CHUNK = 128
GROUP_DIM = 128
N_GROUPS = 8
HEAD_DIM = 128
N_HEADS = 8
D_FF = 4 * D_MODEL
CONV_WIDTH = 3
EPS = 1e-6
N_SEG = 7

V7X_VMEM_BYTES = 64 * 1024 * 1024
V7X_SUBLANES = 8
VMEM_LIMIT = V7X_VMEM_BYTES - 8 * 1024 * 1024

MIX_ROW_TILE = 1024
FFN_ROW_TILE = 512
FFN_SUB_ROWS = 256
FF_TILE = 1024
Q_TILE = 256
K_TILE = 256
HEADS_PER_STEP = 4
TAIL = V7X_SUBLANES
LOG_F32_UNDERFLOW = -104.0


def _rms(x):
    return x * lax.rsqrt(jnp.mean(x * x, axis=-1, keepdims=True) + EPS)


def _gelu_erf(x):
    return 0.5 * x * (1.0 + lax.erf(x * (1.0 / math.sqrt(2.0))))


GELU_C1 = math.sqrt(2.0 / math.pi)
GELU_C3 = 0.044715 * GELU_C1


def _sigmoid(x):
    return 1.0 / (1.0 + jnp.exp(-x))


def _adaln_kernel(c_ref, w_ref, b_ref, o_ref):
    c = c_ref[...]
    c_act = c * _sigmoid(c)
    o_ref[...] = jnp.sum(c_act * w_ref[...], axis=0, keepdims=True) + b_ref[...]


def _adaln_mod(c, w_ada, b_ada):
    depth, d, n = w_ada.shape
    tn = 1536
    return pl.pallas_call(
        _adaln_kernel,
        grid=(depth, n // tn),
        in_specs=[
            pl.BlockSpec((d, 1), lambda l, j: (0, 0)),
            pl.BlockSpec((None, d, tn), lambda l, j: (l, 0, j)),
            pl.BlockSpec((None, 1, tn), lambda l, j: (l, 0, j)),
        ],
        out_specs=pl.BlockSpec((None, 1, tn), lambda l, j: (l, 0, j)),
        out_shape=jax.ShapeDtypeStruct((depth, 1, n), F32),
        compiler_params=pltpu.CompilerParams(
            dimension_semantics=("arbitrary", "arbitrary"), vmem_limit_bytes=VMEM_LIMIT),
        name="adaln_mod",
    )(c.reshape(d, 1), w_ada, b_ada.reshape(depth, 1, n))


SEG_GATE_A, SEG_GATE_B, SEG_U, SEG_V, SEG_Q, SEG_K, SEG_VB = range(N_SEG)
FIRST_SEG_COLUMN = 5


def _mixer_in_kernel(x_ref, mod_ref, gpre_ref, w_ref, gsgu_ref, wsp_ref, bsp_ref, wa_ref,
                     q_ref, k_ref, v_ref, yag_ref, gb_ref, h_s, u_s, vn_s):
    j = pl.program_id(1)
    tm = x_ref.shape[0]

    @pl.when(j == 0)
    def _():
        hn = _rms(x_ref[...]) * gpre_ref[...]
        h = hn * (1.0 + mod_ref[1:2, :]) + mod_ref[0:1, :]
        h_s[...] = h.astype(BF16)

    def proj():
        return jnp.dot(h_s[...], w_ref[...], preferred_element_type=F32)

    @pl.when(j == SEG_GATE_A)
    def _():
        yag_ref[...] = _sigmoid(proj()).astype(BF16)

    @pl.when(j == SEG_GATE_B)
    def _():
        gb_ref[...] = _sigmoid(proj()).astype(BF16)

    @pl.when(j == SEG_U)
    def _():
        u_s[...] = _gelu_erf(proj())

    @pl.when(j == SEG_V)
    def _():
        causal = (lax.broadcasted_iota(jnp.int32, (CHUNK, CHUNK), 0)
                  >= lax.broadcasted_iota(jnp.int32, (CHUNK, CHUNK), 1))
        w_sp = [jnp.where(causal, wsp_ref[g], 0.0).astype(BF16) for g in range(N_GROUPS)]
        half = tm // 2
        halves = [slice(r * half, (r + 1) * half) for r in range(2)]
        proj_v = [jnp.dot(h_s[rows, :], w_ref[...], preferred_element_type=F32) for rows in halves]
        for r, rows in enumerate(halves):
            vg = _gelu_erf(proj_v[r])
            mu = jnp.mean(vg, axis=-1, keepdims=True)
            vc = vg - mu
            var = jnp.mean(vc * vc, axis=-1, keepdims=True)
            vn_s[rows, :] = (vc * lax.rsqrt(var + EPS) * gsgu_ref[...]).astype(BF16)
            for g in range(N_GROUPS):
                cols = slice(g * GROUP_DIM, (g + 1) * GROUP_DIM)
                bg = bsp_ref[:, g:g + 1]
                for c in range(r * half // CHUNK, (r + 1) * half // CHUNK):
                    blk = slice(c * CHUNK, (c + 1) * CHUNK)
                    f = jnp.dot(w_sp[g], vn_s[blk, cols], preferred_element_type=F32) + bg
                    vn_s[blk, cols] = (u_s[blk, cols] * f).astype(BF16)
        for rows in halves:
            y_a = jnp.dot(vn_s[rows, :], wa_ref[...], preferred_element_type=F32)
            yag_ref[rows, :] = (yag_ref[rows, :].astype(F32) * y_a).astype(BF16)

    @pl.when(j == SEG_Q)
    def _():
        q_ref[...] = (proj() * (HEAD_DIM ** -0.5)).astype(BF16)

    @pl.when(j == SEG_K)
    def _():
        k_ref[...] = proj().astype(BF16)

    @pl.when(j == SEG_VB)
    def _():
        v_ref[...] = proj().astype(BF16)


def _mixer_in(l, x, mod, g_pre, w_in, g_sgu, w_sp, b_sp_t, w_a):
    s, d = x.shape
    tm = MIX_ROW_TILE
    vec_spec = pl.BlockSpec((None, 1, d), lambda i, j: (l, 0, 0))
    out = jax.ShapeDtypeStruct((s, d), BF16)

    def out_spec(first_seg):
        return pl.BlockSpec(
            (tm, d), lambda i, j: (jnp.where(j >= first_seg, i, jnp.maximum(i - 1, 0)), 0))

    return pl.pallas_call(
        _mixer_in_kernel,
        grid=(s // tm, N_SEG),
        in_specs=[
            pl.BlockSpec(
                (tm, d), lambda i, j: (jnp.where(j >= 1, jnp.minimum(i + 1, s // tm - 1), i), 0)),
            pl.BlockSpec((None, N_MOD, d), lambda i, j: (l, 0, 0)),
            vec_spec,
            pl.BlockSpec((None, d, d), lambda i, j: (l, 0, (j + FIRST_SEG_COLUMN) % N_SEG)),
            vec_spec,
            pl.BlockSpec((None, N_GROUPS, CHUNK, CHUNK), lambda i, j: (l, 0, 0, 0)),
            pl.BlockSpec((None, CHUNK, N_GROUPS), lambda i, j: (l, 0, 0)),
            pl.BlockSpec((None, d, d), lambda i, j: (l, 0, 0), pipeline_mode=pl.Buffered(1)),
        ],
        out_specs=[out_spec(SEG_Q), out_spec(SEG_K), out_spec(SEG_VB), out_spec(SEG_GATE_A),
                   out_spec(SEG_GATE_B)],
        out_shape=[out] * 5,
        scratch_shapes=[
            pltpu.VMEM((tm, d), BF16),
            pltpu.VMEM((tm, d), F32),
            pltpu.VMEM((tm, d), BF16),
        ],
        compiler_params=pltpu.CompilerParams(
            dimension_semantics=("arbitrary", "arbitrary"), vmem_limit_bytes=VMEM_LIMIT),
        name="mixer_in",
    )(x, mod, g_pre, w_in, g_sgu, w_sp, b_sp_t, w_a)


def _sb_attn_kernel(q_ref, k_ref, v_ref, o_ref, acc_s, carry_s):
    i = pl.program_id(1)
    tq = q_ref.shape[0]
    tk = K_TILE
    row = lax.broadcasted_iota(jnp.int32, (tq, tk), 0)
    col = lax.broadcasted_iota(jnp.int32, (tq, tk), 1)
    strictly_causal = col < row
    tri = jnp.where(lax.broadcasted_iota(jnp.int32, (tk, tk), 0)
                    >= lax.broadcasted_iota(jnp.int32, (tk, tk), 1), 1.0, 0.0).astype(BF16)

    def key_block(kb, diagonal):
        start = pl.multiple_of(kb * tk, tk)
        heads = range(HEADS_PER_STEP)
        lanes = [slice(h * HEAD_DIM, (h + 1) * HEAD_DIM) for h in heads]
        z = [lax.dot_general(q_ref[:, lanes[h]], k_ref[pl.ds(start, tk), lanes[h]],
                             (((1,), (1,)), ((), ())), preferred_element_type=F32)
             for h in heads]
        suffix = []
        for h in heads:
            softplus = jnp.maximum(z[h], 0.0) + jnp.log(1.0 + jnp.exp(-jnp.abs(z[h])))
            if diagonal:
                softplus = jnp.where(strictly_causal, softplus, 0.0)
            hi = softplus.astype(BF16)
            lo = (softplus - hi.astype(F32)).astype(BF16)
            both = jnp.dot(jnp.concatenate([hi, lo], axis=0), tri, preferred_element_type=F32)
            suffix.append(both[:tq] + both[tq:])
        for h in heads:
            carry = carry_s[h]
            attn = jnp.exp(z[h] - suffix[h] - carry)
            if diagonal:
                attn = jnp.where(strictly_causal, attn, 0.0)
            acc_s[:, lanes[h]] += jnp.dot(attn.astype(BF16), v_ref[pl.ds(start, tk), lanes[h]],
                                          preferred_element_type=F32)
            carry_s[h] = carry + suffix[h][:, 0:1]

    acc_s[...] = jnp.zeros_like(acc_s)
    carry_s[...] = jnp.zeros_like(carry_s)
    key_block(i, True)

    def cond(state):
        kb, live = state
        return jnp.logical_and(kb >= 0, live)

    def body(state):
        kb, _ = state
        key_block(kb, False)
        return kb - 1, jnp.min(carry_s[...]) < -LOG_F32_UNDERFLOW

    lax.while_loop(cond, body, (i - 1, True))
    o_ref[...] = acc_s[...].astype(o_ref.dtype)


def _sb_attn(q, k, v):
    s, d = q.shape
    tq = Q_TILE
    assert Q_TILE == K_TILE and N_HEADS % HEADS_PER_STEP == 0
    width = HEADS_PER_STEP * HEAD_DIM
    blk = pl.BlockSpec((tq, width), lambda h, i: (i, h))
    full = pl.BlockSpec((s, width), lambda h, i: (0, h), pipeline_mode=pl.Buffered(1))
    return pl.pallas_call(
        _sb_attn_kernel,
        grid=(N_HEADS // HEADS_PER_STEP, s // tq),
        in_specs=[blk, full, full],
        out_specs=blk,
        out_shape=jax.ShapeDtypeStruct((s, d), BF16),
        scratch_shapes=[pltpu.VMEM((tq, width), F32),
                        pltpu.VMEM((HEADS_PER_STEP, tq, 1), F32)],
        compiler_params=pltpu.CompilerParams(
            dimension_semantics=("arbitrary", "arbitrary"), vmem_limit_bytes=VMEM_LIMIT),
        name="sb_attn",
    )(q, k, v)


def _mix_ffn_kernel(x_ref, attn_ref, yag_ref, gb_ref, mod_ref, gpost_ref, gpre_ref, gpost2_ref,
                    wb_ref, wo_ref, wua_ref, wub_ref, cw_a_ref, cw_b_ref, cb_a_ref, cb_b_ref,
                    wd_ref, o_ref, x1_s, h_s, acc_s, tail_a_s, tail_b_s):
    i = pl.program_id(0)
    c = pl.program_id(1)
    tm = x_ref.shape[0]
    sub = FFN_SUB_ROWS
    blocks = [slice(r * sub, (r + 1) * sub) for r in range(tm // sub)]

    @pl.when(c == 0)
    def _():
        y_b = [jnp.dot(attn_ref[rows, :], wb_ref[...], preferred_element_type=F32)
               for rows in blocks]
        y = []
        for r, rows in enumerate(blocks):
            merged = yag_ref[rows, :].astype(F32) + gb_ref[rows, :].astype(F32) * y_b[r]
            y.append(jnp.dot(merged.astype(BF16), wo_ref[...], preferred_element_type=F32))
        for r, rows in enumerate(blocks):
            x1 = x_ref[rows, :] + (1.0 + mod_ref[2:3, :]) * (_rms(y[r]) * gpost_ref[...])
            x1_s[rows, :] = x1
            hn = _rms(x1) * gpre_ref[...]
            h_s[rows, :] = (hn * (1.0 + mod_ref[4:5, :]) + mod_ref[3:4, :]).astype(BF16)
        acc_s[...] = jnp.zeros_like(acc_s)

    @pl.when(i == 0)
    def _():
        tail_a_s[c] = jnp.zeros(tail_a_s.shape[1:], F32)
        tail_b_s[c] = jnp.zeros(tail_b_s.shape[1:], F32)

    def up(rows):
        hr = h_s[rows, :]
        return (jnp.dot(hr, wua_ref[...], preferred_element_type=F32),
                jnp.dot(hr, wub_ref[...], preferred_element_type=F32))

    def conv(prev_tail, u, cw, cb):
        ext = jnp.concatenate([prev_tail, u], axis=0)
        out = cb
        for kk in range(CONV_WIDTH):
            shift = CONV_WIDTH - 1 - kk
            out = out + cw[kk:kk + 1, :] * ext[TAIL - shift:TAIL - shift + sub, :]
        return out

    cw_a, cb_a = cw_a_ref[...], cb_a_ref[...]
    cw_b, cb_b = 0.5 * cw_b_ref[...], 0.5 * cb_b_ref[...]

    def down(rows, tails, ua, ub):
        a = conv(tails[0], ua, cw_a, cb_a)
        b_half = conv(tails[1], ub, cw_b, cb_b)
        t = jnp.tanh(a * (GELU_C1 + GELU_C3 * (a * a)))
        act = (a + a * t) * b_half
        acc_s[rows, :] += jnp.dot(act.astype(BF16), wd_ref[...], preferred_element_type=F32)

    tails = (tail_a_s[c], tail_b_s[c])
    ups = up(blocks[0])
    for r, rows in enumerate(blocks):
        nxt = up(blocks[r + 1]) if r + 1 < len(blocks) else None
        down(rows, tails, *ups)
        tails = (ups[0][sub - TAIL:, :], ups[1][sub - TAIL:, :])
        ups = nxt
    tail_a_s[c] = tails[0]
    tail_b_s[c] = tails[1]

    @pl.when(c == pl.num_programs(1) - 1)
    def _():
        o_ref[...] = x1_s[...] + (1.0 + mod_ref[5:6, :]) * (_rms(acc_s[...]) * gpost2_ref[...])


def _mix_ffn(l, x, attn, yag, gb, mod, g_post, g_pre2, g_post2, w_b, w_o, w_up, conv_w, conv_b,
             w_down):
    s, d = x.shape
    tm, tc = FFN_ROW_TILE, FF_TILE
    nc = D_FF // tc
    row_spec = pl.BlockSpec((tm, d), lambda i, c: (i, 0))
    in_row_spec = pl.BlockSpec(
        (tm, d), lambda i, c: (jnp.where(c >= 1, jnp.minimum(i + 1, s // tm - 1), i), 0))
    vec_spec = pl.BlockSpec((None, 1, d), lambda i, c: (l, 0, 0))
    sq_spec = pl.BlockSpec((None, d, d), lambda i, c: (l, 0, 0), pipeline_mode=pl.Buffered(1))
    return pl.pallas_call(
        _mix_ffn_kernel,
        grid=(s // tm, nc),
        in_specs=[
            in_row_spec, in_row_spec, in_row_spec, in_row_spec,
            pl.BlockSpec((None, N_MOD, d), lambda i, c: (l, 0, 0)),
            vec_spec, vec_spec, vec_spec,
            sq_spec, sq_spec,
            pl.BlockSpec((None, d, tc), lambda i, c: (l, 0, c)),
            pl.BlockSpec((None, d, tc), lambda i, c: (l, 0, nc + c)),
            pl.BlockSpec((None, CONV_WIDTH, tc), lambda i, c: (l, 0, c)),
            pl.BlockSpec((None, CONV_WIDTH, tc), lambda i, c: (l, 0, nc + c)),
            pl.BlockSpec((None, 1, tc), lambda i, c: (l, 0, c)),
            pl.BlockSpec((None, 1, tc), lambda i, c: (l, 0, nc + c)),
            pl.BlockSpec((None, tc, d), lambda i, c: (l, c, 0)),
        ],
        out_specs=row_spec,
        out_shape=jax.ShapeDtypeStruct((s, d), F32),
        scratch_shapes=[
            pltpu.VMEM((tm, d), F32),
            pltpu.VMEM((tm, d), BF16),
            pltpu.VMEM((tm, d), F32),
            pltpu.VMEM((nc, TAIL, tc), F32),
            pltpu.VMEM((nc, TAIL, tc), F32),
        ],
        compiler_params=pltpu.CompilerParams(
            dimension_semantics=("arbitrary", "arbitrary"), vmem_limit_bytes=VMEM_LIMIT),
        name="mix_ffn",
    )(x, attn, yag, gb, mod, g_post, g_pre2, g_post2, w_b, w_o, w_up, w_up, conv_w, conv_w,
      conv_b, conv_b, w_down)


def kernel(x, c, w_ada, b_ada, g_pre_mix, g_post_mix, g_pre_ffn, g_post_ffn, w_in, g_sgu,
           w_spatial, b_spatial, w_a_out, w_b_out, w_o, w_up, conv_w, conv_b, w_down):
    batch, s, d = x.shape
    depth = w_in.shape[0]
    assert batch == 1 and d == D_MODEL
    assert s % MIX_ROW_TILE == 0 and s % FFN_ROW_TILE == 0 and s % Q_TILE == 0
    mod = _adaln_mod(c, w_ada, b_ada).reshape(depth, N_MOD, d)
    vec = lambda g: g.reshape(depth, 1, -1)
    w_in, w_a_out, w_b_out, w_o, w_up, w_down = (
        w.astype(BF16) for w in (w_in, w_a_out, w_b_out, w_o, w_up, w_down))
    b_sp_t = jnp.transpose(b_spatial, (0, 2, 1))
    xs = x.reshape(s, d)
    for l in range(depth):
        q, k, v, yag, gb = _mixer_in(l, xs, mod, vec(g_pre_mix), w_in, vec(g_sgu), w_spatial,
                                     b_sp_t, w_a_out)
        attn = _sb_attn(q, k, v)
        xs = _mix_ffn(l, xs, attn, yag, gb, mod, vec(g_post_mix), vec(g_pre_ffn),
                      vec(g_post_ffn), w_b_out, w_o, w_up, conv_w, vec(conv_b), w_down)
    return xs.reshape(batch, s, d)
```

```python
import math

import jax
import jax.numpy as jnp
from jax import lax
from jax.experimental import pallas as pl
from jax.experimental.pallas import tpu as pltpu

F32 = jnp.float32
BF16 = jnp.bfloat16

D_MODEL = 1024
N_MOD = 6
CHUNK = 128
GROUP_DIM = 128
N_GROUPS = 8
HEAD_DIM = 128
N_HEADS = 8
D_FF = 4 * D_MODEL
CONV_WIDTH = 3
EPS = 1e-6
N_SEG = 7

V7X_VMEM_BYTES = 64 * 1024 * 1024
V7X_SUBLANES = 8
VMEM_LIMIT = V7X_VMEM_BYTES - 8 * 1024 * 1024

MIX_ROW_TILE = 1024
FFN_ROW_TILE = 512
FFN_SUB_ROWS = 256
FF_TILE = 1024
Q_TILE = 256
K_TILE = 256
HEADS_PER_STEP = 4
TAIL = V7X_SUBLANES
LOG_F32_UNDERFLOW = -104.0


def _rms(x):
    return x * lax.rsqrt(jnp.mean(x * x, axis=-1, keepdims=True) + EPS)


def _gelu_erf(x):
    return 0.5 * x * (1.0 + lax.erf(x * (1.0 / math.sqrt(2.0))))


GELU_C1 = math.sqrt(2.0 / math.pi)
GELU_C3 = 0.044715 * GELU_C1


def _sigmoid(x):
    return 1.0 / (1.0 + jnp.exp(-x))


def _pack_weight(w):
    depth, k, n = w.shape
    pairs = jnp.swapaxes(w.astype(BF16).reshape(depth, k // 2, 2, n), -1, -2)
    return lax.bitcast_convert_type(pairs, jnp.uint32)


def _weight(ref):
    return pltpu.bitcast(ref[...], BF16)


def _adaln_kernel(c_ref, w_ref, b_ref, o_ref):
    c = c_ref[...]
    c_act = c * _sigmoid(c)
    o_ref[...] = jnp.sum(c_act * w_ref[...], axis=0, keepdims=True) + b_ref[...]


def _adaln_mod(c, w_ada, b_ada):
    depth, d, n = w_ada.shape
    tn = 1536
    return pl.pallas_call(
        _adaln_kernel,
        grid=(depth, n // tn),
        in_specs=[
            pl.BlockSpec((d, 1), lambda l, j: (0, 0)),
            pl.BlockSpec((None, d, tn), lambda l, j: (l, 0, j)),
            pl.BlockSpec((None, 1, tn), lambda l, j: (l, 0, j)),
        ],
        out_specs=pl.BlockSpec((None, 1, tn), lambda l, j: (l, 0, j)),
        out_shape=jax.ShapeDtypeStruct((depth, 1, n), F32),
        compiler_params=pltpu.CompilerParams(
            dimension_semantics=("arbitrary", "arbitrary"), vmem_limit_bytes=VMEM_LIMIT),
        name="adaln_mod",
    )(c.reshape(d, 1), w_ada, b_ada.reshape(depth, 1, n))


SEG_GATE_A, SEG_GATE_B, SEG_U, SEG_V, SEG_Q, SEG_K, SEG_VB = range(N_SEG)
FIRST_SEG_COLUMN = 5


def _mixer_in_kernel(x_ref, mod_ref, gpre_ref, w_ref, gsgu_ref, wsp_ref, bsp_ref, wa_ref,
                     q_ref, k_ref, v_ref, yag_ref, gb_ref, h_s, u_s, vn_s):
    j = pl.program_id(1)
    tm = x_ref.shape[0]

    @pl.when(j == 0)
    def _():
        hn = _rms(x_ref[...]) * gpre_ref[...]
        h = hn * (1.0 + mod_ref[1:2, :]) + mod_ref[0:1, :]
        h_s[...] = h.astype(BF16)

    def proj():
        return jnp.dot(h_s[...], _weight(w_ref), preferred_element_type=F32)

    @pl.when(j == SEG_GATE_A)
    def _():
        yag_ref[...] = _sigmoid(proj()).astype(BF16)

    @pl.when(j == SEG_GATE_B)
    def _():
        gb_ref[...] = _sigmoid(proj()).astype(BF16)

    @pl.when(j == SEG_U)
    def _():
        u_s[...] = _gelu_erf(proj())

    @pl.when(j == SEG_V)
    def _():
        causal = (lax.broadcasted_iota(jnp.int32, (CHUNK, CHUNK), 0)
                  >= lax.broadcasted_iota(jnp.int32, (CHUNK, CHUNK), 1))
        w_sp = [jnp.where(causal, wsp_ref[g], 0.0).astype(BF16) for g in range(N_GROUPS)]
        half = tm // 2
        halves = [slice(r * half, (r + 1) * half) for r in range(2)]
        proj_v = [jnp.dot(h_s[rows, :], _weight(w_ref), preferred_element_type=F32) for rows in halves]
        for r, rows in enumerate(halves):
            vg = _gelu_erf(proj_v[r])
            mu = jnp.mean(vg, axis=-1, keepdims=True)
            vc = vg - mu
            var = jnp.mean(vc * vc, axis=-1, keepdims=True)
            vn_s[rows, :] = (vc * lax.rsqrt(var + EPS) * gsgu_ref[...]).astype(BF16)
            for g in range(N_GROUPS):
                cols = slice(g * GROUP_DIM, (g + 1) * GROUP_DIM)
                bg = bsp_ref[:, g:g + 1]
                for c in range(r * half // CHUNK, (r + 1) * half // CHUNK):
                    blk = slice(c * CHUNK, (c + 1) * CHUNK)
                    f = jnp.dot(w_sp[g], vn_s[blk, cols], preferred_element_type=F32) + bg
                    vn_s[blk, cols] = (u_s[blk, cols] * f).astype(BF16)
        for rows in halves:
            y_a = jnp.dot(vn_s[rows, :], _weight(wa_ref), preferred_element_type=F32)
            yag_ref[rows, :] = (yag_ref[rows, :].astype(F32) * y_a).astype(BF16)

    @pl.when(j == SEG_Q)
    def _():
        q_ref[...] = (proj() * (HEAD_DIM ** -0.5)).astype(BF16)

    @pl.when(j == SEG_K)
    def _():
        k_ref[...] = proj().astype(BF16)

    @pl.when(j == SEG_VB)
    def _():
        v_ref[...] = proj().astype(BF16)


def _mixer_in(l, x, mod, g_pre, w_in, g_sgu, w_sp, b_sp_t, w_a):
    s, d = x.shape
    tm = MIX_ROW_TILE
    vec_spec = pl.BlockSpec((None, 1, d), lambda i, j: (l, 0, 0))
    out = jax.ShapeDtypeStruct((s, d), BF16)

    def out_spec(first_seg):
        return pl.BlockSpec(
            (tm, d), lambda i, j: (jnp.where(j >= first_seg, i, jnp.maximum(i - 1, 0)), 0))

    return pl.pallas_call(
        _mixer_in_kernel,
        grid=(s // tm, N_SEG),
        in_specs=[
            pl.BlockSpec(
                (tm, d), lambda i, j: (jnp.where(j >= 1, jnp.minimum(i + 1, s // tm - 1), i), 0)),
            pl.BlockSpec((None, N_MOD, d), lambda i, j: (l, 0, 0)),
            vec_spec,
            pl.BlockSpec((None, d // 2, d), lambda i, j: (l, 0, (j + FIRST_SEG_COLUMN) % N_SEG)),
            vec_spec,
            pl.BlockSpec((None, N_GROUPS, CHUNK, CHUNK), lambda i, j: (l, 0, 0, 0)),
            pl.BlockSpec((None, CHUNK, N_GROUPS), lambda i, j: (l, 0, 0)),
            pl.BlockSpec((None, d // 2, d), lambda i, j: (l, 0, 0), pipeline_mode=pl.Buffered(1)),
        ],
        out_specs=[out_spec(SEG_Q), out_spec(SEG_K), out_spec(SEG_VB), out_spec(SEG_GATE_A),
                   out_spec(SEG_GATE_B)],
        out_shape=[out] * 5,
        scratch_shapes=[
            pltpu.VMEM((tm, d), BF16),
            pltpu.VMEM((tm, d), F32),
            pltpu.VMEM((tm, d), BF16),
        ],
        compiler_params=pltpu.CompilerParams(
            dimension_semantics=("arbitrary", "arbitrary"), vmem_limit_bytes=VMEM_LIMIT),
        name="mixer_in",
    )(x, mod, g_pre, w_in, g_sgu, w_sp, b_sp_t, w_a)


def _sb_attn_kernel(q_ref, k_ref, v_ref, o_ref, acc_s, carry_s):
    i = pl.program_id(1)
    tq = q_ref.shape[0]
    tk = K_TILE
    row = lax.broadcasted_iota(jnp.int32, (tq, tk), 0)
    col = lax.broadcasted_iota(jnp.int32, (tq, tk), 1)
    strictly_causal = col < row
    tri = jnp.where(lax.broadcasted_iota(jnp.int32, (tk, tk), 0)
                    >= lax.broadcasted_iota(jnp.int32, (tk, tk), 1), 1.0, 0.0).astype(BF16)

    def key_block(kb, diagonal):
        start = pl.multiple_of(kb * tk, tk)
        heads = range(HEADS_PER_STEP)
        lanes = [slice(h * HEAD_DIM, (h + 1) * HEAD_DIM) for h in heads]
        z = [lax.dot_general(q_ref[:, lanes[h]], k_ref[pl.ds(start, tk), lanes[h]],
                             (((1,), (1,)), ((), ())), preferred_element_type=F32)
             for h in heads]
        suffix = []
        for h in heads:
            softplus = jnp.maximum(z[h], 0.0) + jnp.log(1.0 + jnp.exp(-jnp.abs(z[h])))
            if diagonal:
                softplus = jnp.where(strictly_causal, softplus, 0.0)
            hi = softplus.astype(BF16)
            lo = (softplus - hi.astype(F32)).astype(BF16)
            both = jnp.dot(jnp.concatenate([hi, lo], axis=0), tri, preferred_element_type=F32)
            suffix.append(both[:tq] + both[tq:])
        for h in heads:
            carry = carry_s[h]
            attn = jnp.exp(z[h] - suffix[h] - carry)
            if diagonal:
                attn = jnp.where(strictly_causal, attn, 0.0)
            acc_s[:, lanes[h]] += jnp.dot(attn.astype(BF16), v_ref[pl.ds(start, tk), lanes[h]],
                                          preferred_element_type=F32)
            carry_s[h] = carry + suffix[h][:, 0:1]

    acc_s[...] = jnp.zeros_like(acc_s)
    carry_s[...] = jnp.zeros_like(carry_s)
    key_block(i, True)

    def cond(state):
        kb, live = state
        return jnp.logical_and(kb >= 0, live)

    def body(state):
        kb, _ = state
        key_block(kb, False)
        return kb - 1, jnp.min(carry_s[...]) < -LOG_F32_UNDERFLOW

    lax.while_loop(cond, body, (i - 1, True))
    o_ref[...] = acc_s[...].astype(o_ref.dtype)


def _sb_attn(q, k, v):
    s, d = q.shape
    tq = Q_TILE
    assert Q_TILE == K_TILE and N_HEADS % HEADS_PER_STEP == 0
    width = HEADS_PER_STEP * HEAD_DIM
    blk = pl.BlockSpec((tq, width), lambda h, i: (i, h))
    full = pl.BlockSpec((s, width), lambda h, i: (0, h), pipeline_mode=pl.Buffered(1))
    return pl.pallas_call(
        _sb_attn_kernel,
        grid=(N_HEADS // HEADS_PER_STEP, s // tq),
        in_specs=[blk, full, full],
        out_specs=blk,
        out_shape=jax.ShapeDtypeStruct((s, d), BF16),
        scratch_shapes=[pltpu.VMEM((tq, width), F32),
                        pltpu.VMEM((HEADS_PER_STEP, tq, 1), F32)],
        compiler_params=pltpu.CompilerParams(
            dimension_semantics=("arbitrary", "arbitrary"), vmem_limit_bytes=VMEM_LIMIT),
        name="sb_attn",
    )(q, k, v)


def _mix_ffn_kernel(x_ref, attn_ref, yag_ref, gb_ref, mod_ref, gpost_ref, gpre_ref, gpost2_ref,
                    wb_ref, wo_ref, wua_ref, wub_ref, cw_a_ref, cw_b_ref, cb_a_ref, cb_b_ref,
                    wd_ref, o_ref, x1_s, h_s, acc_s, tail_a_s, tail_b_s):
    i = pl.program_id(0)
    c = pl.program_id(1)
    tm = x_ref.shape[0]
    sub = FFN_SUB_ROWS
    blocks = [slice(r * sub, (r + 1) * sub) for r in range(tm // sub)]

    @pl.when(c == 0)
    def _():
        y_b = [jnp.dot(attn_ref[rows, :], _weight(wb_ref), preferred_element_type=F32)
               for rows in blocks]
        y = []
        for r, rows in enumerate(blocks):
            merged = yag_ref[rows, :].astype(F32) + gb_ref[rows, :].astype(F32) * y_b[r]
            y.append(jnp.dot(merged.astype(BF16), _weight(wo_ref), preferred_element_type=F32))
        for r, rows in enumerate(blocks):
            x1 = x_ref[rows, :] + (1.0 + mod_ref[2:3, :]) * (_rms(y[r]) * gpost_ref[...])
            x1_s[rows, :] = x1
            hn = _rms(x1) * gpre_ref[...]
            h_s[rows, :] = (hn * (1.0 + mod_ref[4:5, :]) + mod_ref[3:4, :]).astype(BF16)
        acc_s[...] = jnp.zeros_like(acc_s)

    @pl.when(i == 0)
    def _():
        tail_a_s[c] = jnp.zeros(tail_a_s.shape[1:], F32)
        tail_b_s[c] = jnp.zeros(tail_b_s.shape[1:], F32)

    def up(rows):
        hr = h_s[rows, :]
        return (jnp.dot(hr, _weight(wua_ref), preferred_element_type=F32),
                jnp.dot(hr, _weight(wub_ref), preferred_element_type=F32))

    def conv(prev_tail, u, cw, cb):
        ext = jnp.concatenate([prev_tail, u], axis=0)
        out = cb
        for kk in range(CONV_WIDTH):
            shift = CONV_WIDTH - 1 - kk
            out = out + cw[kk:kk + 1, :] * ext[TAIL - shift:TAIL - shift + sub, :]
        return out

    cw_a, cb_a = cw_a_ref[...], cb_a_ref[...]
    cw_b, cb_b = 0.5 * cw_b_ref[...], 0.5 * cb_b_ref[...]

    def down(rows, tails, ua, ub):
        a = conv(tails[0], ua, cw_a, cb_a)
        b_half = conv(tails[1], ub, cw_b, cb_b)
        t = jnp.tanh(a * (GELU_C1 + GELU_C3 * (a * a)))
        act = (a + a * t) * b_half
        acc_s[rows, :] += jnp.dot(act.astype(BF16), _weight(wd_ref), preferred_element_type=F32)

    tails = (tail_a_s[c], tail_b_s[c])
    ups = up(blocks[0])
    for r, rows in enumerate(blocks):
        nxt = up(blocks[r + 1]) if r + 1 < len(blocks) else None
        down(rows, tails, *ups)
        tails = (ups[0][sub - TAIL:, :], ups[1][sub - TAIL:, :])
        ups = nxt
    tail_a_s[c] = tails[0]
    tail_b_s[c] = tails[1]

    @pl.when(c == pl.num_programs(1) - 1)
    def _():
        o_ref[...] = x1_s[...] + (1.0 + mod_ref[5:6, :]) * (_rms(acc_s[...]) * gpost2_ref[...])


def _mix_ffn(l, x, attn, yag, gb, mod, g_post, g_pre2, g_post2, w_b, w_o, w_up, conv_w, conv_b,
             w_down):
    s, d = x.shape
    tm, tc = FFN_ROW_TILE, FF_TILE
    nc = D_FF // tc
    row_spec = pl.BlockSpec((tm, d), lambda i, c: (i, 0))
    in_row_spec = pl.BlockSpec(
        (tm, d), lambda i, c: (jnp.where(c >= 1, jnp.minimum(i + 1, s // tm - 1), i), 0))
    vec_spec = pl.BlockSpec((None, 1, d), lambda i, c: (l, 0, 0))
    sq_spec = pl.BlockSpec((None, d // 2, d), lambda i, c: (l, 0, 0), pipeline_mode=pl.Buffered(1))
    return pl.pallas_call(
        _mix_ffn_kernel,
        grid=(s // tm, nc),
        in_specs=[
            in_row_spec, in_row_spec, in_row_spec, in_row_spec,
            pl.BlockSpec((None, N_MOD, d), lambda i, c: (l, 0, 0)),
            vec_spec, vec_spec, vec_spec,
            sq_spec, sq_spec,
            pl.BlockSpec((None, d // 2, tc), lambda i, c: (l, 0, c)),
            pl.BlockSpec((None, d // 2, tc), lambda i, c: (l, 0, nc + c)),
            pl.BlockSpec((None, CONV_WIDTH, tc), lambda i, c: (l, 0, c)),
            pl.BlockSpec((None, CONV_WIDTH, tc), lambda i, c: (l, 0, nc + c)),
            pl.BlockSpec((None, 1, tc), lambda i, c: (l, 0, c)),
            pl.BlockSpec((None, 1, tc), lambda i, c: (l, 0, nc + c)),
            pl.BlockSpec((None, tc // 2, d), lambda i, c: (l, c, 0)),
        ],
        out_specs=row_spec,
        out_shape=jax.ShapeDtypeStruct((s, d), F32),
        scratch_shapes=[
            pltpu.VMEM((tm, d), F32),
            pltpu.VMEM((tm, d), BF16),
            pltpu.VMEM((tm, d), F32),
            pltpu.VMEM((nc, TAIL, tc), F32),
            pltpu.VMEM((nc, TAIL, tc), F32),
        ],
        compiler_params=pltpu.CompilerParams(
            dimension_semantics=("arbitrary", "arbitrary"), vmem_limit_bytes=VMEM_LIMIT),
        name="mix_ffn",
    )(x, attn, yag, gb, mod, g_post, g_pre2, g_post2, w_b, w_o, w_up, w_up, conv_w, conv_w,
      conv_b, conv_b, w_down)


def kernel(x, c, w_ada, b_ada, g_pre_mix, g_post_mix, g_pre_ffn, g_post_ffn, w_in, g_sgu,
           w_spatial, b_spatial, w_a_out, w_b_out, w_o, w_up, conv_w, conv_b, w_down):
    batch, s, d = x.shape
    depth = w_in.shape[0]
    assert batch == 1 and d == D_MODEL
    assert s % MIX_ROW_TILE == 0 and s % FFN_ROW_TILE == 0 and s % Q_TILE == 0
    mod = _adaln_mod(c, w_ada, b_ada).reshape(depth, N_MOD, d)
    vec = lambda g: g.reshape(depth, 1, -1)
    w_in, w_a_out, w_b_out, w_o, w_up, w_down = (
        _pack_weight(w) for w in (w_in, w_a_out, w_b_out, w_o, w_up, w_down))
    b_sp_t = jnp.transpose(b_spatial, (0, 2, 1))
    xs = x.reshape(s, d)
    for l in range(depth):
        q, k, v, yag, gb = _mixer_in(l, xs, mod, vec(g_pre_mix), w_in, vec(g_sgu), w_spatial,
                                     b_sp_t, w_a_out)
        attn = _sb_attn(q, k, v)
        xs = _mix_ffn(l, xs, attn, yag, gb, mod, vec(g_post_mix), vec(g_pre_ffn),
                      vec(g_post_ffn), w_b_out, w_o, w_up, conv_w, vec(conv_b), w_down)
    return xs.reshape(batch, s, d)
```

```python
import math

import jax
import jax.numpy as jnp
from jax import lax
from jax.experimental import pallas as pl
from jax.experimental.pallas import tpu as pltpu

F32 = jnp.float32
BF16 = jnp.bfloat16

D_MODEL = 1024
N_MOD = 6
CHUNK = 128
GROUP_DIM = 128
N_GROUPS = 8
HEAD_DIM = 128
N_HEADS = 8
D_FF = 4 * D_MODEL
CONV_WIDTH = 3
EPS = 1e-6
N_SEG = 7

V7X_VMEM_BYTES = 64 * 1024 * 1024
V7X_SUBLANES = 8
VMEM_LIMIT = V7X_VMEM_BYTES - 8 * 1024 * 1024

MIX_ROW_TILE = 1024
FFN_ROW_TILE = 512
FFN_SUB_ROWS = 256
FF_TILE = 1024
Q_TILE = 256
K_TILE = 256
HEADS_PER_STEP = 4
PACK_TILE = (1024, 1024)
TAIL = V7X_SUBLANES
LOG_F32_UNDERFLOW = -104.0


def _rms(x):
    return x * lax.rsqrt(jnp.mean(x * x, axis=-1, keepdims=True) + EPS)


def _gelu_erf(x):
    return 0.5 * x * (1.0 + lax.erf(x * (1.0 / math.sqrt(2.0))))


GELU_C1 = math.sqrt(2.0 / math.pi)
GELU_C3 = 0.044715 * GELU_C1


def _sigmoid(x):
    return 1.0 / (1.0 + jnp.exp(-x))


def _pack_kernel(w_ref, o_ref):
    o_ref[...] = pltpu.bitcast(w_ref[...].astype(BF16), jnp.uint32)


def _pack_weight(w):
    depth, k, n = w.shape
    tk, tn = PACK_TILE
    return pl.pallas_call(
        _pack_kernel,
        grid=(depth, k // tk, n // tn),
        in_specs=[pl.BlockSpec((None, tk, tn), lambda l, i, j: (l, i, j))],
        out_specs=pl.BlockSpec((None, tk // 2, tn), lambda l, i, j: (l, i, j)),
        out_shape=jax.ShapeDtypeStruct((depth, k // 2, n), jnp.uint32),
        compiler_params=pltpu.CompilerParams(
            dimension_semantics=("arbitrary",) * 3, vmem_limit_bytes=VMEM_LIMIT),
        name="pack_weight",
    )(w)


def _weight(ref):
    return pltpu.bitcast(ref[...], BF16)


def _adaln_kernel(c_ref, w_ref, b_ref, o_ref):
    c = c_ref[...]
    c_act = c * _sigmoid(c)
    o_ref[...] = jnp.sum(c_act * w_ref[...], axis=0, keepdims=True) + b_ref[...]


def _adaln_mod(c, w_ada, b_ada):
    depth, d, n = w_ada.shape
    tn = 1536
    return pl.pallas_call(
        _adaln_kernel,
        grid=(depth, n // tn),
        in_specs=[
            pl.BlockSpec((d, 1), lambda l, j: (0, 0)),
            pl.BlockSpec((None, d, tn), lambda l, j: (l, 0, j)),
            pl.BlockSpec((None, 1, tn), lambda l, j: (l, 0, j)),
        ],
        out_specs=pl.BlockSpec((None, 1, tn), lambda l, j: (l, 0, j)),
        out_shape=jax.ShapeDtypeStruct((depth, 1, n), F32),
        compiler_params=pltpu.CompilerParams(
            dimension_semantics=("arbitrary", "arbitrary"), vmem_limit_bytes=VMEM_LIMIT),
        name="adaln_mod",
    )(c.reshape(d, 1), w_ada, b_ada.reshape(depth, 1, n))


SEG_GATE_A, SEG_GATE_B, SEG_U, SEG_V, SEG_Q, SEG_K, SEG_VB = range(N_SEG)
FIRST_SEG_COLUMN = 5


def _mixer_in_kernel(x_ref, mod_ref, gpre_ref, w_ref, gsgu_ref, wsp_ref, bsp_ref, wa_ref,
                     q_ref, k_ref, v_ref, yag_ref, gb_ref, h_s, u_s, vn_s):
    j = pl.program_id(1)
    tm = x_ref.shape[0]

    @pl.when(j == 0)
    def _():
        hn = _rms(x_ref[...]) * gpre_ref[...]
        h = hn * (1.0 + mod_ref[1:2, :]) + mod_ref[0:1, :]
        h_s[...] = h.astype(BF16)

    def proj():
        return jnp.dot(h_s[...], _weight(w_ref), preferred_element_type=F32)

    @pl.when(j == SEG_GATE_A)
    def _():
        yag_ref[...] = _sigmoid(proj()).astype(BF16)

    @pl.when(j == SEG_GATE_B)
    def _():
        gb_ref[...] = _sigmoid(proj()).astype(BF16)

    @pl.when(j == SEG_U)
    def _():
        u_s[...] = _gelu_erf(proj())

    @pl.when(j == SEG_V)
    def _():
        causal = (lax.broadcasted_iota(jnp.int32, (CHUNK, CHUNK), 0)
                  >= lax.broadcasted_iota(jnp.int32, (CHUNK, CHUNK), 1))
        w_sp = [jnp.where(causal, wsp_ref[g], 0.0).astype(BF16) for g in range(N_GROUPS)]
        half = tm // 2
        halves = [slice(r * half, (r + 1) * half) for r in range(2)]
        proj_v = [jnp.dot(h_s[rows, :], _weight(w_ref), preferred_element_type=F32) for rows in halves]
        for r, rows in enumerate(halves):
            vg = _gelu_erf(proj_v[r])
            mu = jnp.mean(vg, axis=-1, keepdims=True)
            vc = vg - mu
            var = jnp.mean(vc * vc, axis=-1, keepdims=True)
            vn_s[rows, :] = (vc * lax.rsqrt(var + EPS) * gsgu_ref[...]).astype(BF16)
            for g in range(N_GROUPS):
                cols = slice(g * GROUP_DIM, (g + 1) * GROUP_DIM)
                bg = bsp_ref[:, g:g + 1]
                for c in range(r * half // CHUNK, (r + 1) * half // CHUNK):
                    blk = slice(c * CHUNK, (c + 1) * CHUNK)
                    f = jnp.dot(w_sp[g], vn_s[blk, cols], preferred_element_type=F32) + bg
                    vn_s[blk, cols] = (u_s[blk, cols] * f).astype(BF16)
        for rows in halves:
            y_a = jnp.dot(vn_s[rows, :], _weight(wa_ref), preferred_element_type=F32)
            yag_ref[rows, :] = (yag_ref[rows, :].astype(F32) * y_a).astype(BF16)

    @pl.when(j == SEG_Q)
    def _():
        q_ref[...] = (proj() * (HEAD_DIM ** -0.5)).astype(BF16)

    @pl.when(j == SEG_K)
    def _():
        k_ref[...] = proj().astype(BF16)

    @pl.when(j == SEG_VB)
    def _():
        v_ref[...] = proj().astype(BF16)


def _mixer_in(l, x, mod, g_pre, w_in, g_sgu, w_sp, b_sp_t, w_a):
    s, d = x.shape
    tm = MIX_ROW_TILE
    vec_spec = pl.BlockSpec((None, 1, d), lambda i, j: (l, 0, 0))
    out = jax.ShapeDtypeStruct((s, d), BF16)

    def out_spec(first_seg):
        return pl.BlockSpec(
            (tm, d), lambda i, j: (jnp.where(j >= first_seg, i, jnp.maximum(i - 1, 0)), 0))

    return pl.pallas_call(
        _mixer_in_kernel,
        grid=(s // tm, N_SEG),
        in_specs=[
            pl.BlockSpec(
                (tm, d), lambda i, j: (jnp.where(j >= 1, jnp.minimum(i + 1, s // tm - 1), i), 0)),
            pl.BlockSpec((None, N_MOD, d), lambda i, j: (l, 0, 0)),
            vec_spec,
            pl.BlockSpec((None, d // 2, d), lambda i, j: (l, 0, (j + FIRST_SEG_COLUMN) % N_SEG)),
            vec_spec,
            pl.BlockSpec((None, N_GROUPS, CHUNK, CHUNK), lambda i, j: (l, 0, 0, 0)),
            pl.BlockSpec((None, CHUNK, N_GROUPS), lambda i, j: (l, 0, 0)),
            pl.BlockSpec((None, d // 2, d), lambda i, j: (l, 0, 0), pipeline_mode=pl.Buffered(1)),
        ],
        out_specs=[out_spec(SEG_Q), out_spec(SEG_K), out_spec(SEG_VB), out_spec(SEG_GATE_A),
                   out_spec(SEG_GATE_B)],
        out_shape=[out] * 5,
        scratch_shapes=[
            pltpu.VMEM((tm, d), BF16),
            pltpu.VMEM((tm, d), F32),
            pltpu.VMEM((tm, d), BF16),
        ],
        compiler_params=pltpu.CompilerParams(
            dimension_semantics=("arbitrary", "arbitrary"), vmem_limit_bytes=VMEM_LIMIT),
        name="mixer_in",
    )(x, mod, g_pre, w_in, g_sgu, w_sp, b_sp_t, w_a)


def _sb_attn_kernel(q_ref, k_ref, v_ref, o_ref, acc_s, carry_s):
    i = pl.program_id(1)
    tq = q_ref.shape[0]
    tk = K_TILE
    row = lax.broadcasted_iota(jnp.int32, (tq, tk), 0)
    col = lax.broadcasted_iota(jnp.int32, (tq, tk), 1)
    strictly_causal = col < row
    tri = jnp.where(lax.broadcasted_iota(jnp.int32, (tk, tk), 0)
                    >= lax.broadcasted_iota(jnp.int32, (tk, tk), 1), 1.0, 0.0).astype(BF16)

    def key_block(kb, diagonal):
        start = pl.multiple_of(kb * tk, tk)
        heads = range(HEADS_PER_STEP)
        lanes = [slice(h * HEAD_DIM, (h + 1) * HEAD_DIM) for h in heads]
        z = [lax.dot_general(q_ref[:, lanes[h]], k_ref[pl.ds(start, tk), lanes[h]],
                             (((1,), (1,)), ((), ())), preferred_element_type=F32)
             for h in heads]
        suffix = []
        for h in heads:
            softplus = jnp.maximum(z[h], 0.0) + jnp.log(1.0 + jnp.exp(-jnp.abs(z[h])))
            if diagonal:
                softplus = jnp.where(strictly_causal, softplus, 0.0)
            hi = softplus.astype(BF16)
            lo = (softplus - hi.astype(F32)).astype(BF16)
            both = jnp.dot(jnp.concatenate([hi, lo], axis=0), tri, preferred_element_type=F32)
            suffix.append(both[:tq] + both[tq:])
        for h in heads:
            carry = carry_s[h]
            attn = jnp.exp(z[h] - suffix[h] - carry)
            if diagonal:
                attn = jnp.where(strictly_causal, attn, 0.0)
            acc_s[:, lanes[h]] += jnp.dot(attn.astype(BF16), v_ref[pl.ds(start, tk), lanes[h]],
                                          preferred_element_type=F32)
            carry_s[h] = carry + suffix[h][:, 0:1]

    acc_s[...] = jnp.zeros_like(acc_s)
    carry_s[...] = jnp.zeros_like(carry_s)
    key_block(i, True)

    def cond(state):
        kb, live = state
        return jnp.logical_and(kb >= 0, live)

    def body(state):
        kb, _ = state
        key_block(kb, False)
        return kb - 1, jnp.min(carry_s[...]) < -LOG_F32_UNDERFLOW

    lax.while_loop(cond, body, (i - 1, True))
    o_ref[...] = acc_s[...].astype(o_ref.dtype)


def _sb_attn(q, k, v):
    s, d = q.shape
    tq = Q_TILE
    assert Q_TILE == K_TILE and N_HEADS % HEADS_PER_STEP == 0
    width = HEADS_PER_STEP * HEAD_DIM
    blk = pl.BlockSpec((tq, width), lambda h, i: (i, h))
    full = pl.BlockSpec((s, width), lambda h, i: (0, h), pipeline_mode=pl.Buffered(1))
    return pl.pallas_call(
        _sb_attn_kernel,
        grid=(N_HEADS // HEADS_PER_STEP, s // tq),
        in_specs=[blk, full, full],
        out_specs=blk,
        out_shape=jax.ShapeDtypeStruct((s, d), BF16),
        scratch_shapes=[pltpu.VMEM((tq, width), F32),
                        pltpu.VMEM((HEADS_PER_STEP, tq, 1), F32)],
        compiler_params=pltpu.CompilerParams(
            dimension_semantics=("arbitrary", "arbitrary"), vmem_limit_bytes=VMEM_LIMIT),
        name="sb_attn",
    )(q, k, v)


def _mix_ffn_kernel(x_ref, attn_ref, yag_ref, gb_ref, mod_ref, gpost_ref, gpre_ref, gpost2_ref,
                    wb_ref, wo_ref, wua_ref, wub_ref, cw_a_ref, cw_b_ref, cb_a_ref, cb_b_ref,
                    wd_ref, o_ref, x1_s, h_s, acc_s, tail_a_s, tail_b_s):
    i = pl.program_id(0)
    c = pl.program_id(1)
    tm = x_ref.shape[0]
    sub = FFN_SUB_ROWS
    blocks = [slice(r * sub, (r + 1) * sub) for r in range(tm // sub)]

    @pl.when(c == 0)
    def _():
        y_b = [jnp.dot(attn_ref[rows, :], _weight(wb_ref), preferred_element_type=F32)
               for rows in blocks]
        y = []
        for r, rows in enumerate(blocks):
            merged = yag_ref[rows, :].astype(F32) + gb_ref[rows, :].astype(F32) * y_b[r]
            y.append(jnp.dot(merged.astype(BF16), _weight(wo_ref), preferred_element_type=F32))
        for r, rows in enumerate(blocks):
            x1 = x_ref[rows, :] + (1.0 + mod_ref[2:3, :]) * (_rms(y[r]) * gpost_ref[...])
            x1_s[rows, :] = x1
            hn = _rms(x1) * gpre_ref[...]
            h_s[rows, :] = (hn * (1.0 + mod_ref[4:5, :]) + mod_ref[3:4, :]).astype(BF16)
        acc_s[...] = jnp.zeros_like(acc_s)

    @pl.when(i == 0)
    def _():
        tail_a_s[c] = jnp.zeros(tail_a_s.shape[1:], F32)
        tail_b_s[c] = jnp.zeros(tail_b_s.shape[1:], F32)

    def up(rows):
        hr = h_s[rows, :]
        return (jnp.dot(hr, _weight(wua_ref), preferred_element_type=F32),
                jnp.dot(hr, _weight(wub_ref), preferred_element_type=F32))

    def conv(prev_tail, u, cw, cb):
        ext = jnp.concatenate([prev_tail, u], axis=0)
        out = cb
        for kk in range(CONV_WIDTH):
            shift = CONV_WIDTH - 1 - kk
            out = out + cw[kk:kk + 1, :] * ext[TAIL - shift:TAIL - shift + sub, :]
        return out

    cw_a, cb_a = cw_a_ref[...], cb_a_ref[...]
    cw_b, cb_b = 0.5 * cw_b_ref[...], 0.5 * cb_b_ref[...]

    def down(rows, tails, ua, ub):
        a = conv(tails[0], ua, cw_a, cb_a)
        b_half = conv(tails[1], ub, cw_b, cb_b)
        t = jnp.tanh(a * (GELU_C1 + GELU_C3 * (a * a)))
        act = (a + a * t) * b_half
        acc_s[rows, :] += jnp.dot(act.astype(BF16), _weight(wd_ref), preferred_element_type=F32)

    tails = (tail_a_s[c], tail_b_s[c])
    ups = up(blocks[0])
    for r, rows in enumerate(blocks):
        nxt = up(blocks[r + 1]) if r + 1 < len(blocks) else None
        down(rows, tails, *ups)
        tails = (ups[0][sub - TAIL:, :], ups[1][sub - TAIL:, :])
        ups = nxt
    tail_a_s[c] = tails[0]
    tail_b_s[c] = tails[1]

    @pl.when(c == pl.num_programs(1) - 1)
    def _():
        o_ref[...] = x1_s[...] + (1.0 + mod_ref[5:6, :]) * (_rms(acc_s[...]) * gpost2_ref[...])


def _mix_ffn(l, x, attn, yag, gb, mod, g_post, g_pre2, g_post2, w_b, w_o, w_up, conv_w, conv_b,
             w_down):
    s, d = x.shape
    tm, tc = FFN_ROW_TILE, FF_TILE
    nc = D_FF // tc
    row_spec = pl.BlockSpec((tm, d), lambda i, c: (i, 0))
    in_row_spec = pl.BlockSpec(
        (tm, d), lambda i, c: (jnp.where(c >= 1, jnp.minimum(i + 1, s // tm - 1), i), 0))
    vec_spec = pl.BlockSpec((None, 1, d), lambda i, c: (l, 0, 0))
    sq_spec = pl.BlockSpec((None, d // 2, d), lambda i, c: (l, 0, 0), pipeline_mode=pl.Buffered(1))
    return pl.pallas_call(
        _mix_ffn_kernel,
        grid=(s // tm, nc),
        in_specs=[
            in_row_spec, in_row_spec, in_row_spec, in_row_spec,
            pl.BlockSpec((None, N_MOD, d), lambda i, c: (l, 0, 0)),
            vec_spec, vec_spec, vec_spec,
            sq_spec, sq_spec,
            pl.BlockSpec((None, d // 2, tc), lambda i, c: (l, 0, c)),
            pl.BlockSpec((None, d // 2, tc), lambda i, c: (l, 0, nc + c)),
            pl.BlockSpec((None, CONV_WIDTH, tc), lambda i, c: (l, 0, c)),
            pl.BlockSpec((None, CONV_WIDTH, tc), lambda i, c: (l, 0, nc + c)),
            pl.BlockSpec((None, 1, tc), lambda i, c: (l, 0, c)),
            pl.BlockSpec((None, 1, tc), lambda i, c: (l, 0, nc + c)),
            pl.BlockSpec((None, tc // 2, d), lambda i, c: (l, c, 0)),
        ],
        out_specs=row_spec,
        out_shape=jax.ShapeDtypeStruct((s, d), F32),
        scratch_shapes=[
            pltpu.VMEM((tm, d), F32),
            pltpu.VMEM((tm, d), BF16),
            pltpu.VMEM((tm, d), F32),
            pltpu.VMEM((nc, TAIL, tc), F32),
            pltpu.VMEM((nc, TAIL, tc), F32),
        ],
        compiler_params=pltpu.CompilerParams(
            dimension_semantics=("arbitrary", "arbitrary"), vmem_limit_bytes=VMEM_LIMIT),
        name="mix_ffn",
    )(x, attn, yag, gb, mod, g_post, g_pre2, g_post2, w_b, w_o, w_up, w_up, conv_w, conv_w,
      conv_b, conv_b, w_down)


def kernel(x, c, w_ada, b_ada, g_pre_mix, g_post_mix, g_pre_ffn, g_post_ffn, w_in, g_sgu,
           w_spatial, b_spatial, w_a_out, w_b_out, w_o, w_up, conv_w, conv_b, w_down):
    batch, s, d = x.shape
    depth = w_in.shape[0]
    assert batch == 1 and d == D_MODEL
    assert s % MIX_ROW_TILE == 0 and s % FFN_ROW_TILE == 0 and s % Q_TILE == 0
    mod = _adaln_mod(c, w_ada, b_ada).reshape(depth, N_MOD, d)
    vec = lambda g: g.reshape(depth, 1, -1)
    w_in, w_a_out, w_b_out, w_o, w_up, w_down = (
        _pack_weight(w) for w in (w_in, w_a_out, w_b_out, w_o, w_up, w_down))
    b_sp_t = jnp.transpose(b_spatial, (0, 2, 1))
    xs = x.reshape(s, d)
    for l in range(depth):
        q, k, v, yag, gb = _mixer_in(l, xs, mod, vec(g_pre_mix), w_in, vec(g_sgu), w_spatial,
                                     b_sp_t, w_a_out)
        attn = _sb_attn(q, k, v)
        xs = _mix_ffn(l, xs, attn, yag, gb, mod, vec(g_post_mix), vec(g_pre_ffn),
                      vec(g_post_ffn), w_b_out, w_o, w_up, conv_w, vec(conv_b), w_down)
    return xs.reshape(batch, s, d)
```

```python
import math

import jax
import jax.numpy as jnp
from jax import lax
from jax.experimental import pallas as pl
from jax.experimental.pallas import tpu as pltpu

F32 = jnp.float32
BF16 = jnp.bfloat16

D_MODEL = 1024
N_MOD = 6
CHUNK = 128
GROUP_DIM = 128
N_GROUPS = 8
HEAD_DIM = 128
N_HEADS = 8
D_FF = 4 * D_MODEL
CONV_WIDTH = 3
EPS = 1e-6
N_SEG = 7

V7X_VMEM_BYTES = 64 * 1024 * 1024
V7X_SUBLANES = 8
VMEM_LIMIT = V7X_VMEM_BYTES - 8 * 1024 * 1024

MIX_ROW_TILE = 1024
FFN_ROW_TILE = 512
FFN_SUB_ROWS = 256
FF_TILE = 1024
Q_TILE = 256
K_TILE = 256
PACK_TILE = (1024, 1024)
TAIL = V7X_SUBLANES
LOG_F32_UNDERFLOW = -104.0


def _rms(x):
    return x * lax.rsqrt(jnp.mean(x * x, axis=-1, keepdims=True) + EPS)


def _gelu_erf(x):
    return 0.5 * x * (1.0 + lax.erf(x * (1.0 / math.sqrt(2.0))))


GELU_C1 = math.sqrt(2.0 / math.pi)
GELU_C3 = 0.044715 * GELU_C1


def _sigmoid(x):
    return 1.0 / (1.0 + jnp.exp(-x))


def _pack_kernel(w_ref, o_ref):
    o_ref[...] = pltpu.bitcast(w_ref[...].astype(BF16), jnp.uint32)


def _pack_weight(w):
    depth, k, n = w.shape
    tk, tn = PACK_TILE
    return pl.pallas_call(
        _pack_kernel,
        grid=(depth, k // tk, n // tn),
        in_specs=[pl.BlockSpec((None, tk, tn), lambda l, i, j: (l, i, j))],
        out_specs=pl.BlockSpec((None, tk // 2, tn), lambda l, i, j: (l, i, j)),
        out_shape=jax.ShapeDtypeStruct((depth, k // 2, n), jnp.uint32),
        compiler_params=pltpu.CompilerParams(
            dimension_semantics=("arbitrary",) * 3, vmem_limit_bytes=VMEM_LIMIT),
        name="pack_weight",
    )(w)


def _weight(ref):
    return pltpu.bitcast(ref[...], BF16)


def _adaln_kernel(c_ref, w_ref, b_ref, o_ref):
    c = c_ref[...]
    c_act = c * _sigmoid(c)
    o_ref[...] = jnp.sum(c_act * w_ref[...], axis=0, keepdims=True) + b_ref[...]


def _adaln_mod(c, w_ada, b_ada):
    depth, d, n = w_ada.shape
    tn = 1536
    return pl.pallas_call(
        _adaln_kernel,
        grid=(depth, n // tn),
        in_specs=[
            pl.BlockSpec((d, 1), lambda l, j: (0, 0)),
            pl.BlockSpec((None, d, tn), lambda l, j: (l, 0, j)),
            pl.BlockSpec((None, 1, tn), lambda l, j: (l, 0, j)),
        ],
        out_specs=pl.BlockSpec((None, 1, tn), lambda l, j: (l, 0, j)),
        out_shape=jax.ShapeDtypeStruct((depth, 1, n), F32),
        compiler_params=pltpu.CompilerParams(
            dimension_semantics=("arbitrary", "arbitrary"), vmem_limit_bytes=VMEM_LIMIT),
        name="adaln_mod",
    )(c.reshape(d, 1), w_ada, b_ada.reshape(depth, 1, n))


SEG_GATE_A, SEG_GATE_B, SEG_U, SEG_V, SEG_Q, SEG_K, SEG_VB = range(N_SEG)
FIRST_SEG_COLUMN = 5


def _mixer_in_kernel(x_ref, mod_ref, gpre_ref, w_ref, gsgu_ref, wsp_ref, bsp_ref, wa_ref,
                     q_ref, k_ref, v_ref, yag_ref, gb_ref, h_s, u_s, vn_s):
    j = pl.program_id(1)
    tm = x_ref.shape[0]

    @pl.when(j == 0)
    def _():
        hn = _rms(x_ref[...]) * gpre_ref[...]
        h = hn * (1.0 + mod_ref[1:2, :]) + mod_ref[0:1, :]
        h_s[...] = h.astype(BF16)

    def proj():
        return jnp.dot(h_s[...], _weight(w_ref), preferred_element_type=F32)

    @pl.when(j == SEG_GATE_A)
    def _():
        yag_ref[...] = _sigmoid(proj()).astype(BF16)

    @pl.when(j == SEG_GATE_B)
    def _():
        gb_ref[...] = _sigmoid(proj()).astype(BF16)

    @pl.when(j == SEG_U)
    def _():
        u_s[...] = _gelu_erf(proj())

    @pl.when(j == SEG_V)
    def _():
        causal = (lax.broadcasted_iota(jnp.int32, (CHUNK, CHUNK), 0)
                  >= lax.broadcasted_iota(jnp.int32, (CHUNK, CHUNK), 1))
        w_sp = [jnp.where(causal, wsp_ref[g], 0.0).astype(BF16) for g in range(N_GROUPS)]
        half = tm // 2
        halves = [slice(r * half, (r + 1) * half) for r in range(2)]
        proj_v = [jnp.dot(h_s[rows, :], _weight(w_ref), preferred_element_type=F32) for rows in halves]
        for r, rows in enumerate(halves):
            vg = _gelu_erf(proj_v[r])
            mu = jnp.mean(vg, axis=-1, keepdims=True)
            vc = vg - mu
            var = jnp.mean(vc * vc, axis=-1, keepdims=True)
            vn_s[rows, :] = (vc * lax.rsqrt(var + EPS) * gsgu_ref[...]).astype(BF16)
            for g in range(N_GROUPS):
                cols = slice(g * GROUP_DIM, (g + 1) * GROUP_DIM)
                bg = bsp_ref[:, g:g + 1]
                for c in range(r * half // CHUNK, (r + 1) * half // CHUNK):
                    blk = slice(c * CHUNK, (c + 1) * CHUNK)
                    f = jnp.dot(w_sp[g], vn_s[blk, cols], preferred_element_type=F32) + bg
                    vn_s[blk, cols] = (u_s[blk, cols] * f).astype(BF16)
        for rows in halves:
            y_a = jnp.dot(vn_s[rows, :], _weight(wa_ref), preferred_element_type=F32)
            yag_ref[rows, :] = (yag_ref[rows, :].astype(F32) * y_a).astype(BF16)

    @pl.when(j == SEG_Q)
    def _():
        q_ref[...] = (proj() * (HEAD_DIM ** -0.5)).astype(BF16)

    @pl.when(j == SEG_K)
    def _():
        k_ref[...] = proj().astype(BF16)

    @pl.when(j == SEG_VB)
    def _():
        v_ref[...] = proj().astype(BF16)


def _mixer_in(l, x, mod, g_pre, w_in, g_sgu, w_sp, b_sp_t, w_a):
    s, d = x.shape
    tm = MIX_ROW_TILE
    vec_spec = pl.BlockSpec((None, 1, d), lambda i, j: (l, 0, 0))
    out = jax.ShapeDtypeStruct((s, d), BF16)

    def out_spec(first_seg):
        return pl.BlockSpec(
            (tm, d), lambda i, j: (jnp.where(j >= first_seg, i, jnp.maximum(i - 1, 0)), 0))

    return pl.pallas_call(
        _mixer_in_kernel,
        grid=(s // tm, N_SEG),
        in_specs=[
            pl.BlockSpec(
                (tm, d), lambda i, j: (jnp.where(j >= 1, jnp.minimum(i + 1, s // tm - 1), i), 0)),
            pl.BlockSpec((None, N_MOD, d), lambda i, j: (l, 0, 0)),
            vec_spec,
            pl.BlockSpec((None, d // 2, d), lambda i, j: (l, 0, (j + FIRST_SEG_COLUMN) % N_SEG)),
            vec_spec,
            pl.BlockSpec((None, N_GROUPS, CHUNK, CHUNK), lambda i, j: (l, 0, 0, 0)),
            pl.BlockSpec((None, CHUNK, N_GROUPS), lambda i, j: (l, 0, 0)),
            pl.BlockSpec((None, d // 2, d), lambda i, j: (l, 0, 0), pipeline_mode=pl.Buffered(1)),
        ],
        out_specs=[out_spec(SEG_Q), out_spec(SEG_K), out_spec(SEG_VB), out_spec(SEG_GATE_A),
                   out_spec(SEG_GATE_B)],
        out_shape=[out] * 5,
        scratch_shapes=[
            pltpu.VMEM((tm, d), BF16),
            pltpu.VMEM((tm, d), F32),
            pltpu.VMEM((tm, d), BF16),
        ],
        compiler_params=pltpu.CompilerParams(
            dimension_semantics=("arbitrary", "arbitrary"), vmem_limit_bytes=VMEM_LIMIT),
        name="mixer_in",
    )(x, mod, g_pre, w_in, g_sgu, w_sp, b_sp_t, w_a)


def _sb_attn_kernel(q_ref, kd_ref, vd_ref, kp_ref, vp_ref, k_hbm, v_hbm, o_ref,
                    acc_s, carry_s, kbuf_s, vbuf_s, sem):
    i = pl.program_id(0)
    tq = q_ref.shape[0]
    tk = K_TILE
    row = lax.broadcasted_iota(jnp.int32, (tq, tk), 0)
    col = lax.broadcasted_iota(jnp.int32, (tq, tk), 1)
    strictly_causal = col < row
    tri = jnp.where(lax.broadcasted_iota(jnp.int32, (tk, tk), 0)
                    >= lax.broadcasted_iota(jnp.int32, (tk, tk), 1), 1.0, 0.0).astype(BF16)
    heads = range(N_HEADS)
    lanes = [slice(h * HEAD_DIM, (h + 1) * HEAD_DIM) for h in heads]

    def key_block(k_blk, v_blk, diagonal):
        z = [lax.dot_general(q_ref[:, lanes[h]], k_blk[:, lanes[h]],
                             (((1,), (1,)), ((), ())), preferred_element_type=F32)
             for h in heads]
        suffix = []
        for h in heads:
            softplus = jnp.maximum(z[h], 0.0) + jnp.log(1.0 + jnp.exp(-jnp.abs(z[h])))
            if diagonal:
                softplus = jnp.where(strictly_causal, softplus, 0.0)
            hi = softplus.astype(BF16)
            lo = (softplus - hi.astype(F32)).astype(BF16)
            both = jnp.dot(jnp.concatenate([hi, lo], axis=0), tri, preferred_element_type=F32)
            suffix.append(both[:tq] + both[tq:])
        for h in heads:
            carry = carry_s[h]
            attn = jnp.exp(z[h] - suffix[h] - carry)
            if diagonal:
                attn = jnp.where(strictly_causal, attn, 0.0)
            acc_s[:, lanes[h]] += jnp.dot(attn.astype(BF16), v_blk[:, lanes[h]],
                                          preferred_element_type=F32)
            carry_s[h] = carry + suffix[h][:, 0:1]

    def some_row_live():
        return jnp.min(carry_s[...]) < -LOG_F32_UNDERFLOW

    acc_s[...] = jnp.zeros_like(acc_s)
    carry_s[...] = jnp.zeros_like(carry_s)
    key_block(kd_ref, vd_ref, True)

    @pl.when(i >= 1)
    def _():
        key_block(kp_ref, vp_ref, False)

    def cond(state):
        kb, live = state
        return jnp.logical_and(kb >= 0, live)

    def body(state):
        kb, _ = state
        start = pl.multiple_of(kb * tk, tk)
        copies = [pltpu.make_async_copy(src.at[pl.ds(start, tk), :], dst, sem.at[n])
                  for n, (src, dst) in enumerate(((k_hbm, kbuf_s), (v_hbm, vbuf_s)))]
        for copy in copies:
            copy.start()
        for copy in copies:
            copy.wait()
        key_block(kbuf_s, vbuf_s, False)
        return kb - 1, some_row_live()

    lax.while_loop(cond, body, (i - 2, some_row_live()))
    o_ref[...] = acc_s[...].astype(o_ref.dtype)


def _sb_attn(q, k, v):
    s, d = q.shape
    tq = Q_TILE
    assert Q_TILE == K_TILE
    blk = pl.BlockSpec((tq, d), lambda i: (i, 0))
    prev_blk = pl.BlockSpec((tq, d), lambda i: (jnp.maximum(i - 1, 0), 0))
    hbm = pl.BlockSpec(memory_space=pl.ANY)
    return pl.pallas_call(
        _sb_attn_kernel,
        grid=(s // tq,),
        in_specs=[blk, blk, blk, prev_blk, prev_blk, hbm, hbm],
        out_specs=blk,
        out_shape=jax.ShapeDtypeStruct((s, d), BF16),
        scratch_shapes=[
            pltpu.VMEM((tq, d), F32),
            pltpu.VMEM((N_HEADS, tq, 1), F32),
            pltpu.VMEM((K_TILE, d), BF16),
            pltpu.VMEM((K_TILE, d), BF16),
            pltpu.SemaphoreType.DMA((2,)),
        ],
        compiler_params=pltpu.CompilerParams(
            dimension_semantics=("arbitrary",), vmem_limit_bytes=VMEM_LIMIT),
        name="sb_attn",
    )(q, k, v, k, v, k, v)


def _mix_ffn_kernel(x_ref, attn_ref, yag_ref, gb_ref, mod_ref, gpost_ref, gpre_ref, gpost2_ref,
                    wb_ref, wo_ref, wua_ref, wub_ref, cw_a_ref, cw_b_ref, cb_a_ref, cb_b_ref,
                    wd_ref, o_ref, x1_s, h_s, acc_s, tail_a_s, tail_b_s):
    i = pl.program_id(0)
    c = pl.program_id(1)
    tm = x_ref.shape[0]
    sub = FFN_SUB_ROWS
    blocks = [slice(r * sub, (r + 1) * sub) for r in range(tm // sub)]

    @pl.when(c == 0)
    def _():
        y_b = [jnp.dot(attn_ref[rows, :], _weight(wb_ref), preferred_element_type=F32)
               for rows in blocks]
        y = []
        for r, rows in enumerate(blocks):
            merged = yag_ref[rows, :].astype(F32) + gb_ref[rows, :].astype(F32) * y_b[r]
            y.append(jnp.dot(merged.astype(BF16), _weight(wo_ref), preferred_element_type=F32))
        for r, rows in enumerate(blocks):
            x1 = x_ref[rows, :] + (1.0 + mod_ref[2:3, :]) * (_rms(y[r]) * gpost_ref[...])
            x1_s[rows, :] = x1
            hn = _rms(x1) * gpre_ref[...]
            h_s[rows, :] = (hn * (1.0 + mod_ref[4:5, :]) + mod_ref[3:4, :]).astype(BF16)
        acc_s[...] = jnp.zeros_like(acc_s)

    @pl.when(i == 0)
    def _():
        tail_a_s[c] = jnp.zeros(tail_a_s.shape[1:], F32)
        tail_b_s[c] = jnp.zeros(tail_b_s.shape[1:], F32)

    def up(rows):
        hr = h_s[rows, :]
        return (jnp.dot(hr, _weight(wua_ref), preferred_element_type=F32),
                jnp.dot(hr, _weight(wub_ref), preferred_element_type=F32))

    def conv(prev_tail, u, cw, cb):
        ext = jnp.concatenate([prev_tail, u], axis=0)
        out = cb
        for kk in range(CONV_WIDTH):
            shift = CONV_WIDTH - 1 - kk
            out = out + cw[kk:kk + 1, :] * ext[TAIL - shift:TAIL - shift + sub, :]
        return out

    cw_a, cb_a = cw_a_ref[...], cb_a_ref[...]
    cw_b, cb_b = 0.5 * cw_b_ref[...], 0.5 * cb_b_ref[...]

    def down(rows, tails, ua, ub):
        a = conv(tails[0], ua, cw_a, cb_a)
        b_half = conv(tails[1], ub, cw_b, cb_b)
        t = jnp.tanh(a * (GELU_C1 + GELU_C3 * (a * a)))
        act = (a + a * t) * b_half
        acc_s[rows, :] += jnp.dot(act.astype(BF16), _weight(wd_ref), preferred_element_type=F32)

    tails = (tail_a_s[c], tail_b_s[c])
    ups = up(blocks[0])
    for r, rows in enumerate(blocks):
        nxt = up(blocks[r + 1]) if r + 1 < len(blocks) else None
        down(rows, tails, *ups)
        tails = (ups[0][sub - TAIL:, :], ups[1][sub - TAIL:, :])
        ups = nxt
    tail_a_s[c] = tails[0]
    tail_b_s[c] = tails[1]

    @pl.when(c == pl.num_programs(1) - 1)
    def _():
        o_ref[...] = x1_s[...] + (1.0 + mod_ref[5:6, :]) * (_rms(acc_s[...]) * gpost2_ref[...])


def _mix_ffn(l, x, attn, yag, gb, mod, g_post, g_pre2, g_post2, w_b, w_o, w_up, conv_w, conv_b,
             w_down):
    s, d = x.shape
    tm, tc = FFN_ROW_TILE, FF_TILE
    nc = D_FF // tc
    row_spec = pl.BlockSpec((tm, d), lambda i, c: (i, 0))
    in_row_spec = pl.BlockSpec(
        (tm, d), lambda i, c: (jnp.where(c >= 1, jnp.minimum(i + 1, s // tm - 1), i), 0))
    vec_spec = pl.BlockSpec((None, 1, d), lambda i, c: (l, 0, 0))
    sq_spec = pl.BlockSpec((None, d // 2, d), lambda i, c: (l, 0, 0), pipeline_mode=pl.Buffered(1))
    return pl.pallas_call(
        _mix_ffn_kernel,
        grid=(s // tm, nc),
        in_specs=[
            in_row_spec, in_row_spec, in_row_spec, in_row_spec,
            pl.BlockSpec((None, N_MOD, d), lambda i, c: (l, 0, 0)),
            vec_spec, vec_spec, vec_spec,
            sq_spec, sq_spec,
            pl.BlockSpec((None, d // 2, tc), lambda i, c: (l, 0, c)),
            pl.BlockSpec((None, d // 2, tc), lambda i, c: (l, 0, nc + c)),
            pl.BlockSpec((None, CONV_WIDTH, tc), lambda i, c: (l, 0, c)),
            pl.BlockSpec((None, CONV_WIDTH, tc), lambda i, c: (l, 0, nc + c)),
            pl.BlockSpec((None, 1, tc), lambda i, c: (l, 0, c)),
            pl.BlockSpec((None, 1, tc), lambda i, c: (l, 0, nc + c)),
            pl.BlockSpec((None, tc // 2, d), lambda i, c: (l, c, 0)),
        ],
        out_specs=row_spec,
        out_shape=jax.ShapeDtypeStruct((s, d), F32),
        scratch_shapes=[
            pltpu.VMEM((tm, d), F32),
            pltpu.VMEM((tm, d), BF16),
            pltpu.VMEM((tm, d), F32),
            pltpu.VMEM((nc, TAIL, tc), F32),
            pltpu.VMEM((nc, TAIL, tc), F32),
        ],
        compiler_params=pltpu.CompilerParams(
            dimension_semantics=("arbitrary", "arbitrary"), vmem_limit_bytes=VMEM_LIMIT),
        name="mix_ffn",
    )(x, attn, yag, gb, mod, g_post, g_pre2, g_post2, w_b, w_o, w_up, w_up, conv_w, conv_w,
      conv_b, conv_b, w_down)


def kernel(x, c, w_ada, b_ada, g_pre_mix, g_post_mix, g_pre_ffn, g_post_ffn, w_in, g_sgu,
           w_spatial, b_spatial, w_a_out, w_b_out, w_o, w_up, conv_w, conv_b, w_down):
    batch, s, d = x.shape
    depth = w_in.shape[0]
    assert batch == 1 and d == D_MODEL
    assert s % MIX_ROW_TILE == 0 and s % FFN_ROW_TILE == 0 and s % Q_TILE == 0
    mod = _adaln_mod(c, w_ada, b_ada).reshape(depth, N_MOD, d)
    vec = lambda g: g.reshape(depth, 1, -1)
    w_in, w_a_out, w_b_out, w_o, w_up, w_down = (
        _pack_weight(w) for w in (w_in, w_a_out, w_b_out, w_o, w_up, w_down))
    b_sp_t = jnp.transpose(b_spatial, (0, 2, 1))
    xs = x.reshape(s, d)
    for l in range(depth):
        q, k, v, yag, gb = _mixer_in(l, xs, mod, vec(g_pre_mix), w_in, vec(g_sgu), w_spatial,
                                     b_sp_t, w_a_out)
        attn = _sb_attn(q, k, v)
        xs = _mix_ffn(l, xs, attn, yag, gb, mod, vec(g_post_mix), vec(g_pre_ffn),
                      vec(g_post_ffn), w_b_out, w_o, w_up, conv_w, vec(conv_b), w_down)
    return xs.reshape(batch, s, d)
```

```python
import math

import jax
import jax.numpy as jnp
from jax import lax
from jax.experimental import pallas as pl
from jax.experimental.pallas import tpu as pltpu

F32 = jnp.float32
BF16 = jnp.bfloat16

D_MODEL = 1024
N_MOD = 6
CHUNK = 128
GROUP_DIM = 128
N_GROUPS = 8
HEAD_DIM = 128
N_HEADS = 8
D_FF = 4 * D_MODEL
CONV_WIDTH = 3
EPS = 1e-6
N_SEG = 7

V7X_VMEM_BYTES = 64 * 1024 * 1024
V7X_SUBLANES = 8
VMEM_LIMIT = V7X_VMEM_BYTES - 8 * 1024 * 1024

MIX_ROW_TILE = 1024
FFN_ROW_TILE = 512
FFN_SUB_ROWS = 256
FF_TILE = 2048
Q_TILE = 256
K_TILE = 256
PACK_TILE = (1024, 1024)
TAIL = V7X_SUBLANES
LOG_F32_UNDERFLOW = -104.0


def _rms(x):
    return x * lax.rsqrt(jnp.mean(x * x, axis=-1, keepdims=True) + EPS)


def _gelu_erf(x):
    return 0.5 * x * (1.0 + lax.erf(x * (1.0 / math.sqrt(2.0))))


GELU_C1 = math.sqrt(2.0 / math.pi)
GELU_C3 = 0.044715 * GELU_C1


def _sigmoid(x):
    return 1.0 / (1.0 + jnp.exp(-x))


def _pack_kernel(w_ref, o_ref):
    o_ref[...] = pltpu.bitcast(w_ref[...].astype(BF16), jnp.uint32)


def _pack_weight(w):
    depth, k, n = w.shape
    tk, tn = PACK_TILE
    return pl.pallas_call(
        _pack_kernel,
        grid=(depth, k // tk, n // tn),
        in_specs=[pl.BlockSpec((None, tk, tn), lambda l, i, j: (l, i, j))],
        out_specs=pl.BlockSpec((None, tk // 2, tn), lambda l, i, j: (l, i, j)),
        out_shape=jax.ShapeDtypeStruct((depth, k // 2, n), jnp.uint32),
        compiler_params=pltpu.CompilerParams(
            dimension_semantics=("arbitrary",) * 3, vmem_limit_bytes=VMEM_LIMIT),
        name="pack_weight",
    )(w)


def _weight(ref):
    return pltpu.bitcast(ref[...], BF16)


def _adaln_kernel(c_ref, w_ref, b_ref, o_ref):
    c = c_ref[...]
    c_act = c * _sigmoid(c)
    o_ref[...] = jnp.sum(c_act * w_ref[...], axis=0, keepdims=True) + b_ref[...]


def _adaln_mod(c, w_ada, b_ada):
    depth, d, n = w_ada.shape
    tn = 1536
    return pl.pallas_call(
        _adaln_kernel,
        grid=(depth, n // tn),
        in_specs=[
            pl.BlockSpec((d, 1), lambda l, j: (0, 0)),
            pl.BlockSpec((None, d, tn), lambda l, j: (l, 0, j)),
            pl.BlockSpec((None, 1, tn), lambda l, j: (l, 0, j)),
        ],
        out_specs=pl.BlockSpec((None, 1, tn), lambda l, j: (l, 0, j)),
        out_shape=jax.ShapeDtypeStruct((depth, 1, n), F32),
        compiler_params=pltpu.CompilerParams(
            dimension_semantics=("arbitrary", "arbitrary"), vmem_limit_bytes=VMEM_LIMIT),
        name="adaln_mod",
    )(c.reshape(d, 1), w_ada, b_ada.reshape(depth, 1, n))


SEG_GATE_A, SEG_GATE_B, SEG_U, SEG_V, SEG_Q, SEG_K, SEG_VB = range(N_SEG)
FIRST_SEG_COLUMN = 5


def _mixer_in_kernel(x_ref, mod_ref, gpre_ref, w_ref, gsgu_ref, wsp_ref, bsp_ref, wa_ref,
                     q_ref, k_ref, v_ref, yag_ref, gb_ref, h_s, u_s, vn_s):
    j = pl.program_id(1)
    tm = x_ref.shape[0]

    @pl.when(j == 0)
    def _():
        hn = _rms(x_ref[...]) * gpre_ref[...]
        h = hn * (1.0 + mod_ref[1:2, :]) + mod_ref[0:1, :]
        h_s[...] = h.astype(BF16)

    def proj():
        return jnp.dot(h_s[...], _weight(w_ref), preferred_element_type=F32)

    @pl.when(j == SEG_GATE_A)
    def _():
        yag_ref[...] = _sigmoid(proj()).astype(BF16)

    @pl.when(j == SEG_GATE_B)
    def _():
        gb_ref[...] = _sigmoid(proj()).astype(BF16)

    @pl.when(j == SEG_U)
    def _():
        u_s[...] = _gelu_erf(proj())

    @pl.when(j == SEG_V)
    def _():
        causal = (lax.broadcasted_iota(jnp.int32, (CHUNK, CHUNK), 0)
                  >= lax.broadcasted_iota(jnp.int32, (CHUNK, CHUNK), 1))
        w_sp = [jnp.where(causal, wsp_ref[g], 0.0).astype(BF16) for g in range(N_GROUPS)]
        half = tm // 2
        halves = [slice(r * half, (r + 1) * half) for r in range(2)]
        proj_v = [jnp.dot(h_s[rows, :], _weight(w_ref), preferred_element_type=F32) for rows in halves]
        for r, rows in enumerate(halves):
            vg = _gelu_erf(proj_v[r])
            mu = jnp.mean(vg, axis=-1, keepdims=True)
            vc = vg - mu
            var = jnp.mean(vc * vc, axis=-1, keepdims=True)
            vn_s[rows, :] = (vc * lax.rsqrt(var + EPS) * gsgu_ref[...]).astype(BF16)
            for g in range(N_GROUPS):
                cols = slice(g * GROUP_DIM, (g + 1) * GROUP_DIM)
                bg = bsp_ref[:, g:g + 1]
                for c in range(r * half // CHUNK, (r + 1) * half // CHUNK):
                    blk = slice(c * CHUNK, (c + 1) * CHUNK)
                    f = jnp.dot(w_sp[g], vn_s[blk, cols], preferred_element_type=F32) + bg
                    vn_s[blk, cols] = (u_s[blk, cols] * f).astype(BF16)
        for rows in halves:
            y_a = jnp.dot(vn_s[rows, :], _weight(wa_ref), preferred_element_type=F32)
            yag_ref[rows, :] = (yag_ref[rows, :].astype(F32) * y_a).astype(BF16)

    @pl.when(j == SEG_Q)
    def _():
        q_ref[...] = (proj() * (HEAD_DIM ** -0.5)).astype(BF16)

    @pl.when(j == SEG_K)
    def _():
        k_ref[...] = proj().astype(BF16)

    @pl.when(j == SEG_VB)
    def _():
        v_ref[...] = proj().astype(BF16)


def _mixer_in(l, x, mod, g_pre, w_in, g_sgu, w_sp, b_sp_t, w_a):
    s, d = x.shape
    tm = MIX_ROW_TILE
    vec_spec = pl.BlockSpec((None, 1, d), lambda i, j: (l, 0, 0))
    out = jax.ShapeDtypeStruct((s, d), BF16)

    def out_spec(first_seg):
        return pl.BlockSpec(
            (tm, d), lambda i, j: (jnp.where(j >= first_seg, i, jnp.maximum(i - 1, 0)), 0))

    return pl.pallas_call(
        _mixer_in_kernel,
        grid=(s // tm, N_SEG),
        in_specs=[
            pl.BlockSpec(
                (tm, d), lambda i, j: (jnp.where(j >= 1, jnp.minimum(i + 1, s // tm - 1), i), 0)),
            pl.BlockSpec((None, N_MOD, d), lambda i, j: (l, 0, 0)),
            vec_spec,
            pl.BlockSpec((None, d // 2, d), lambda i, j: (l, 0, (j + FIRST_SEG_COLUMN) % N_SEG)),
            vec_spec,
            pl.BlockSpec((None, N_GROUPS, CHUNK, CHUNK), lambda i, j: (l, 0, 0, 0)),
            pl.BlockSpec((None, CHUNK, N_GROUPS), lambda i, j: (l, 0, 0)),
            pl.BlockSpec((None, d // 2, d), lambda i, j: (l, 0, 0), pipeline_mode=pl.Buffered(1)),
        ],
        out_specs=[out_spec(SEG_Q), out_spec(SEG_K), out_spec(SEG_VB), out_spec(SEG_GATE_A),
                   out_spec(SEG_GATE_B)],
        out_shape=[out] * 5,
        scratch_shapes=[
            pltpu.VMEM((tm, d), BF16),
            pltpu.VMEM((tm, d), F32),
            pltpu.VMEM((tm, d), BF16),
        ],
        compiler_params=pltpu.CompilerParams(
            dimension_semantics=("arbitrary", "arbitrary"), vmem_limit_bytes=VMEM_LIMIT),
        name="mixer_in",
    )(x, mod, g_pre, w_in, g_sgu, w_sp, b_sp_t, w_a)


def _sb_attn_kernel(q_ref, kd_ref, vd_ref, kp_ref, vp_ref, k_hbm, v_hbm, o_ref,
                    acc_s, carry_s, kbuf_s, vbuf_s, sem):
    i = pl.program_id(0)
    tq = q_ref.shape[0]
    tk = K_TILE
    row = lax.broadcasted_iota(jnp.int32, (tq, tk), 0)
    col = lax.broadcasted_iota(jnp.int32, (tq, tk), 1)
    strictly_causal = col < row
    tri = jnp.where(lax.broadcasted_iota(jnp.int32, (tk, tk), 0)
                    >= lax.broadcasted_iota(jnp.int32, (tk, tk), 1), 1.0, 0.0).astype(BF16)
    heads = range(N_HEADS)
    lanes = [slice(h * HEAD_DIM, (h + 1) * HEAD_DIM) for h in heads]

    def key_block(k_blk, v_blk, diagonal):
        z = [lax.dot_general(q_ref[:, lanes[h]], k_blk[:, lanes[h]],
                             (((1,), (1,)), ((), ())), preferred_element_type=F32)
             for h in heads]
        suffix = []
        for h in heads:
            softplus = jnp.maximum(z[h], 0.0) + jnp.log(1.0 + jnp.exp(-jnp.abs(z[h])))
            if diagonal:
                softplus = jnp.where(strictly_causal, softplus, 0.0)
            hi = softplus.astype(BF16)
            lo = (softplus - hi.astype(F32)).astype(BF16)
            both = jnp.dot(jnp.concatenate([hi, lo], axis=0), tri, preferred_element_type=F32)
            suffix.append(both[:tq] + both[tq:])
        for h in heads:
            carry = carry_s[h]
            attn = jnp.exp(z[h] - suffix[h] - carry)
            if diagonal:
                attn = jnp.where(strictly_causal, attn, 0.0)
            acc_s[:, lanes[h]] += jnp.dot(attn.astype(BF16), v_blk[:, lanes[h]],
                                          preferred_element_type=F32)
            carry_s[h] = carry + suffix[h][:, 0:1]

    def some_row_live():
        return jnp.min(carry_s[...]) < -LOG_F32_UNDERFLOW

    acc_s[...] = jnp.zeros_like(acc_s)
    carry_s[...] = jnp.zeros_like(carry_s)
    key_block(kd_ref, vd_ref, True)

    @pl.when(i >= 1)
    def _():
        key_block(kp_ref, vp_ref, False)

    def cond(state):
        kb, live = state
        return jnp.logical_and(kb >= 0, live)

    def body(state):
        kb, _ = state
        start = pl.multiple_of(kb * tk, tk)
        copies = [pltpu.make_async_copy(src.at[pl.ds(start, tk), :], dst, sem.at[n])
                  for n, (src, dst) in enumerate(((k_hbm, kbuf_s), (v_hbm, vbuf_s)))]
        for copy in copies:
            copy.start()
        for copy in copies:
            copy.wait()
        key_block(kbuf_s, vbuf_s, False)
        return kb - 1, some_row_live()

    lax.while_loop(cond, body, (i - 2, some_row_live()))
    o_ref[...] = acc_s[...].astype(o_ref.dtype)


def _sb_attn(q, k, v):
    s, d = q.shape
    tq = Q_TILE
    assert Q_TILE == K_TILE
    blk = pl.BlockSpec((tq, d), lambda i: (i, 0))
    prev_blk = pl.BlockSpec((tq, d), lambda i: (jnp.maximum(i - 1, 0), 0))
    hbm = pl.BlockSpec(memory_space=pl.ANY)
    return pl.pallas_call(
        _sb_attn_kernel,
        grid=(s // tq,),
        in_specs=[blk, blk, blk, prev_blk, prev_blk, hbm, hbm],
        out_specs=blk,
        out_shape=jax.ShapeDtypeStruct((s, d), BF16),
        scratch_shapes=[
            pltpu.VMEM((tq, d), F32),
            pltpu.VMEM((N_HEADS, tq, 1), F32),
            pltpu.VMEM((K_TILE, d), BF16),
            pltpu.VMEM((K_TILE, d), BF16),
            pltpu.SemaphoreType.DMA((2,)),
        ],
        compiler_params=pltpu.CompilerParams(
            dimension_semantics=("arbitrary",), vmem_limit_bytes=VMEM_LIMIT),
        name="sb_attn",
    )(q, k, v, k, v, k, v)


def _mix_ffn_kernel(x_ref, attn_ref, yag_ref, gb_ref, mod_ref, gpost_ref, gpre_ref, gpost2_ref,
                    wb_ref, wo_ref, wua_ref, wub_ref, cw_a_ref, cw_b_ref, cb_a_ref, cb_b_ref,
                    wd_ref, o_ref, x1_s, h_s, acc_s, tail_a_s, tail_b_s):
    i = pl.program_id(0)
    c = pl.program_id(1)
    tm = x_ref.shape[0]
    sub = FFN_SUB_ROWS
    blocks = [slice(r * sub, (r + 1) * sub) for r in range(tm // sub)]

    @pl.when(c == 0)
    def _():
        y_b = [jnp.dot(attn_ref[rows, :], _weight(wb_ref), preferred_element_type=F32)
               for rows in blocks]
        y = []
        for r, rows in enumerate(blocks):
            merged = yag_ref[rows, :].astype(F32) + gb_ref[rows, :].astype(F32) * y_b[r]
            y.append(jnp.dot(merged.astype(BF16), _weight(wo_ref), preferred_element_type=F32))
        for r, rows in enumerate(blocks):
            x1 = x_ref[rows, :] + (1.0 + mod_ref[2:3, :]) * (_rms(y[r]) * gpost_ref[...])
            x1_s[rows, :] = x1
            hn = _rms(x1) * gpre_ref[...]
            h_s[rows, :] = (hn * (1.0 + mod_ref[4:5, :]) + mod_ref[3:4, :]).astype(BF16)
        acc_s[...] = jnp.zeros_like(acc_s)

    @pl.when(i == 0)
    def _():
        tail_a_s[c] = jnp.zeros(tail_a_s.shape[1:], F32)
        tail_b_s[c] = jnp.zeros(tail_b_s.shape[1:], F32)

    def up(rows):
        hr = h_s[rows, :]
        return (jnp.dot(hr, _weight(wua_ref), preferred_element_type=F32),
                jnp.dot(hr, _weight(wub_ref), preferred_element_type=F32))

    def conv(prev_tail, u, cw, cb):
        ext = jnp.concatenate([prev_tail, u], axis=0)
        out = cb
        for kk in range(CONV_WIDTH):
            shift = CONV_WIDTH - 1 - kk
            out = out + cw[kk:kk + 1, :] * ext[TAIL - shift:TAIL - shift + sub, :]
        return out

    cw_a, cb_a = cw_a_ref[...], cb_a_ref[...]
    cw_b, cb_b = 0.5 * cw_b_ref[...], 0.5 * cb_b_ref[...]

    def down(rows, tails, ua, ub):
        a = conv(tails[0], ua, cw_a, cb_a)
        b_half = conv(tails[1], ub, cw_b, cb_b)
        t = jnp.tanh(a * (GELU_C1 + GELU_C3 * (a * a)))
        act = (a + a * t) * b_half
        acc_s[rows, :] += jnp.dot(act.astype(BF16), _weight(wd_ref), preferred_element_type=F32)

    tails = (tail_a_s[c], tail_b_s[c])
    ups = up(blocks[0])
    for r, rows in enumerate(blocks):
        nxt = up(blocks[r + 1]) if r + 1 < len(blocks) else None
        down(rows, tails, *ups)
        tails = (ups[0][sub - TAIL:, :], ups[1][sub - TAIL:, :])
        ups = nxt
    tail_a_s[c] = tails[0]
    tail_b_s[c] = tails[1]

    @pl.when(c == pl.num_programs(1) - 1)
    def _():
        o_ref[...] = x1_s[...] + (1.0 + mod_ref[5:6, :]) * (_rms(acc_s[...]) * gpost2_ref[...])


def _mix_ffn(l, x, attn, yag, gb, mod, g_post, g_pre2, g_post2, w_b, w_o, w_up, conv_w, conv_b,
             w_down):
    s, d = x.shape
    tm, tc = FFN_ROW_TILE, FF_TILE
    nc = D_FF // tc
    row_spec = pl.BlockSpec((tm, d), lambda i, c: (i, 0))
    in_row_spec = pl.BlockSpec(
        (tm, d), lambda i, c: (jnp.where(c >= 1, jnp.minimum(i + 1, s // tm - 1), i), 0))
    vec_spec = pl.BlockSpec((None, 1, d), lambda i, c: (l, 0, 0))
    sq_spec = pl.BlockSpec((None, d // 2, d), lambda i, c: (l, 0, 0), pipeline_mode=pl.Buffered(1))
    return pl.pallas_call(
        _mix_ffn_kernel,
        grid=(s // tm, nc),
        in_specs=[
            in_row_spec, in_row_spec, in_row_spec, in_row_spec,
            pl.BlockSpec((None, N_MOD, d), lambda i, c: (l, 0, 0)),
            vec_spec, vec_spec, vec_spec,
            sq_spec, sq_spec,
            pl.BlockSpec((None, d // 2, tc), lambda i, c: (l, 0, c)),
            pl.BlockSpec((None, d // 2, tc), lambda i, c: (l, 0, nc + c)),
            pl.BlockSpec((None, CONV_WIDTH, tc), lambda i, c: (l, 0, c)),
            pl.BlockSpec((None, CONV_WIDTH, tc), lambda i, c: (l, 0, nc + c)),
            pl.BlockSpec((None, 1, tc), lambda i, c: (l, 0, c)),
            pl.BlockSpec((None, 1, tc), lambda i, c: (l, 0, nc + c)),
            pl.BlockSpec((None, tc // 2, d), lambda i, c: (l, c, 0)),
        ],
        out_specs=row_spec,
        out_shape=jax.ShapeDtypeStruct((s, d), F32),
        scratch_shapes=[
            pltpu.VMEM((tm, d), F32),
            pltpu.VMEM((tm, d), BF16),
            pltpu.VMEM((tm, d), F32),
            pltpu.VMEM((nc, TAIL, tc), F32),
            pltpu.VMEM((nc, TAIL, tc), F32),
        ],
        compiler_params=pltpu.CompilerParams(
            dimension_semantics=("arbitrary", "arbitrary"), vmem_limit_bytes=VMEM_LIMIT),
        name="mix_ffn",
    )(x, attn, yag, gb, mod, g_post, g_pre2, g_post2, w_b, w_o, w_up, w_up, conv_w, conv_w,
      conv_b, conv_b, w_down)


def kernel(x, c, w_ada, b_ada, g_pre_mix, g_post_mix, g_pre_ffn, g_post_ffn, w_in, g_sgu,
           w_spatial, b_spatial, w_a_out, w_b_out, w_o, w_up, conv_w, conv_b, w_down):
    batch, s, d = x.shape
    depth = w_in.shape[0]
    assert batch == 1 and d == D_MODEL
    assert s % MIX_ROW_TILE == 0 and s % FFN_ROW_TILE == 0 and s % Q_TILE == 0
    mod = _adaln_mod(c, w_ada, b_ada).reshape(depth, N_MOD, d)
    vec = lambda g: g.reshape(depth, 1, -1)
    w_in, w_a_out, w_b_out, w_o, w_up, w_down = (
        _pack_weight(w) for w in (w_in, w_a_out, w_b_out, w_o, w_up, w_down))
    b_sp_t = jnp.transpose(b_spatial, (0, 2, 1))
    xs = x.reshape(s, d)
    for l in range(depth):
        q, k, v, yag, gb = _mixer_in(l, xs, mod, vec(g_pre_mix), w_in, vec(g_sgu), w_spatial,
                                     b_sp_t, w_a_out)
        attn = _sb_attn(q, k, v)
        xs = _mix_ffn(l, xs, attn, yag, gb, mod, vec(g_post_mix), vec(g_pre_ffn),
                      vec(g_post_ffn), w_b_out, w_o, w_up, conv_w, vec(conv_b), w_down)
    return xs.reshape(batch, s, d)
```

```python
import math

import jax
import jax.numpy as jnp
from jax import lax
from jax.experimental import pallas as pl
from jax.experimental.pallas import tpu as pltpu

F32 = jnp.float32
BF16 = jnp.bfloat16

D_MODEL = 1024
N_MOD = 6
CHUNK = 128
GROUP_DIM = 128
N_GROUPS = 8
HEAD_DIM = 128
N_HEADS = 8
D_FF = 4 * D_MODEL
CONV_WIDTH = 3
EPS = 1e-6
N_SEG = 7

V7X_VMEM_BYTES = 64 * 1024 * 1024
V7X_SUBLANES = 8
VMEM_LIMIT = V7X_VMEM_BYTES - 8 * 1024 * 1024

MIX_ROW_TILE = 1024
FFN_ROW_TILE = 512
FFN_SUB_ROWS = 256
FF_TILE = 2048
Q_TILE = 256
K_TILE = 256
PACK_TILE = (1024, 1024)
TAIL = V7X_SUBLANES
LOG_F32_UNDERFLOW = -104.0


def _rms(x):
    return x * lax.rsqrt(jnp.mean(x * x, axis=-1, keepdims=True) + EPS)


def _gelu_erf(x):
    return 0.5 * x * (1.0 + lax.erf(x * (1.0 / math.sqrt(2.0))))


GELU_C1 = math.sqrt(2.0 / math.pi)
GELU_C3 = 0.044715 * GELU_C1


def _sigmoid(x):
    return 1.0 / (1.0 + jnp.exp(-x))


def _pack_kernel(w_ref, o_ref):
    o_ref[...] = pltpu.bitcast(w_ref[...].astype(BF16), jnp.uint32)


def _pack_weight(w):
    depth, k, n = w.shape
    tk, tn = PACK_TILE
    return pl.pallas_call(
        _pack_kernel,
        grid=(depth, k // tk, n // tn),
        in_specs=[pl.BlockSpec((None, tk, tn), lambda l, i, j: (l, i, j))],
        out_specs=pl.BlockSpec((None, tk // 2, tn), lambda l, i, j: (l, i, j)),
        out_shape=jax.ShapeDtypeStruct((depth, k // 2, n), jnp.uint32),
        compiler_params=pltpu.CompilerParams(
            dimension_semantics=("arbitrary",) * 3, vmem_limit_bytes=VMEM_LIMIT),
        name="pack_weight",
    )(w)


def _weight(ref):
    return pltpu.bitcast(ref[...], BF16)


def _adaln_kernel(c_ref, w_ref, b_ref, o_ref):
    c = c_ref[...]
    c_act = c * _sigmoid(c)
    o_ref[...] = jnp.sum(c_act * w_ref[...], axis=0, keepdims=True) + b_ref[...]


def _adaln_mod(c, w_ada, b_ada):
    depth, d, n = w_ada.shape
    tn = 1536
    return pl.pallas_call(
        _adaln_kernel,
        grid=(depth, n // tn),
        in_specs=[
            pl.BlockSpec((d, 1), lambda l, j: (0, 0)),
            pl.BlockSpec((None, d, tn), lambda l, j: (l, 0, j)),
            pl.BlockSpec((None, 1, tn), lambda l, j: (l, 0, j)),
        ],
        out_specs=pl.BlockSpec((None, 1, tn), lambda l, j: (l, 0, j)),
        out_shape=jax.ShapeDtypeStruct((depth, 1, n), F32),
        compiler_params=pltpu.CompilerParams(
            dimension_semantics=("arbitrary", "arbitrary"), vmem_limit_bytes=VMEM_LIMIT),
        name="adaln_mod",
    )(c.reshape(d, 1), w_ada, b_ada.reshape(depth, 1, n))


SEG_GATE_A, SEG_GATE_B, SEG_U, SEG_V, SEG_Q, SEG_K, SEG_VB = range(N_SEG)
FIRST_SEG_COLUMN = 5


def _mixer_in_kernel(x_ref, mod_ref, gpre_ref, w_ref, gsgu_ref, wsp_ref, bsp_ref, wa_ref,
                     q_ref, k_ref, v_ref, yag_ref, gb_ref, h_s, u_s, vn_s):
    j = pl.program_id(1)
    tm = x_ref.shape[0]

    @pl.when(j == 0)
    def _():
        hn = _rms(x_ref[...]) * gpre_ref[...]
        h = hn * (1.0 + mod_ref[1:2, :]) + mod_ref[0:1, :]
        h_s[...] = h.astype(BF16)

    def proj():
        return jnp.dot(h_s[...], _weight(w_ref), preferred_element_type=F32)

    @pl.when(j == SEG_GATE_A)
    def _():
        yag_ref[...] = _sigmoid(proj()).astype(BF16)

    @pl.when(j == SEG_GATE_B)
    def _():
        gb_ref[...] = _sigmoid(proj()).astype(BF16)

    @pl.when(j == SEG_U)
    def _():
        u_s[...] = _gelu_erf(proj())

    @pl.when(j == SEG_V)
    def _():
        causal = (lax.broadcasted_iota(jnp.int32, (CHUNK, CHUNK), 0)
                  >= lax.broadcasted_iota(jnp.int32, (CHUNK, CHUNK), 1))
        w_sp = [jnp.where(causal, wsp_ref[g], 0.0).astype(BF16) for g in range(N_GROUPS)]
        half = tm // 2
        halves = [slice(r * half, (r + 1) * half) for r in range(2)]
        proj_v = [jnp.dot(h_s[rows, :], _weight(w_ref), preferred_element_type=F32) for rows in halves]
        for r, rows in enumerate(halves):
            vg = _gelu_erf(proj_v[r])
            mu = jnp.mean(vg, axis=-1, keepdims=True)
            vc = vg - mu
            var = jnp.mean(vc * vc, axis=-1, keepdims=True)
            vn_s[rows, :] = (vc * lax.rsqrt(var + EPS) * gsgu_ref[...]).astype(BF16)
            for g in range(N_GROUPS):
                cols = slice(g * GROUP_DIM, (g + 1) * GROUP_DIM)
                bg = bsp_ref[:, g:g + 1]
                for c in range(r * half // CHUNK, (r + 1) * half // CHUNK):
                    blk = slice(c * CHUNK, (c + 1) * CHUNK)
                    f = jnp.dot(w_sp[g], vn_s[blk, cols], preferred_element_type=F32) + bg
                    vn_s[blk, cols] = (u_s[blk, cols] * f).astype(BF16)
        for rows in halves:
            y_a = jnp.dot(vn_s[rows, :], _weight(wa_ref), preferred_element_type=F32)
            yag_ref[rows, :] = (yag_ref[rows, :].astype(F32) * y_a).astype(BF16)

    @pl.when(j == SEG_Q)
    def _():
        q_ref[...] = (proj() * (HEAD_DIM ** -0.5)).astype(BF16)

    @pl.when(j == SEG_K)
    def _():
        k_ref[...] = proj().astype(BF16)

    @pl.when(j == SEG_VB)
    def _():
        v_ref[...] = proj().astype(BF16)


def _mixer_in(l, x, mod, g_pre, w_in, g_sgu, w_sp, b_sp_t, w_a):
    s, d = x.shape
    tm = MIX_ROW_TILE
    vec_spec = pl.BlockSpec((None, 1, d), lambda i, j: (l, 0, 0))
    out = jax.ShapeDtypeStruct((s, d), BF16)

    def out_spec(first_seg):
        return pl.BlockSpec(
            (tm, d), lambda i, j: (jnp.where(j >= first_seg, i, jnp.maximum(i - 1, 0)), 0))

    return pl.pallas_call(
        _mixer_in_kernel,
        grid=(s // tm, N_SEG),
        in_specs=[
            pl.BlockSpec(
                (tm, d), lambda i, j: (jnp.where(j >= 1, jnp.minimum(i + 1, s // tm - 1), i), 0)),
            pl.BlockSpec((None, N_MOD, d), lambda i, j: (l, 0, 0)),
            vec_spec,
            pl.BlockSpec((None, d // 2, d), lambda i, j: (l, 0, (j + FIRST_SEG_COLUMN) % N_SEG)),
            vec_spec,
            pl.BlockSpec((None, N_GROUPS, CHUNK, CHUNK), lambda i, j: (l, 0, 0, 0)),
            pl.BlockSpec((None, CHUNK, N_GROUPS), lambda i, j: (l, 0, 0)),
            pl.BlockSpec((None, d // 2, d), lambda i, j: (l, 0, 0), pipeline_mode=pl.Buffered(1)),
        ],
        out_specs=[out_spec(SEG_Q), out_spec(SEG_K), out_spec(SEG_VB), out_spec(SEG_GATE_A),
                   out_spec(SEG_GATE_B)],
        out_shape=[out] * 5,
        scratch_shapes=[
            pltpu.VMEM((tm, d), BF16),
            pltpu.VMEM((tm, d), F32),
            pltpu.VMEM((tm, d), BF16),
        ],
        compiler_params=pltpu.CompilerParams(
            dimension_semantics=("arbitrary", "arbitrary"), vmem_limit_bytes=VMEM_LIMIT),
        name="mixer_in",
    )(x, mod, g_pre, w_in, g_sgu, w_sp, b_sp_t, w_a)


def _sb_attn_kernel(q_ref, kd_ref, vd_ref, kp_ref, vp_ref, k_hbm, v_hbm, o_ref,
                    acc_s, carry_s, kbuf_s, vbuf_s, sem):
    i = pl.program_id(0)
    tq = q_ref.shape[0]
    tk = K_TILE
    row = lax.broadcasted_iota(jnp.int32, (tq, tk), 0)
    col = lax.broadcasted_iota(jnp.int32, (tq, tk), 1)
    strictly_causal = col < row
    tri = jnp.where(lax.broadcasted_iota(jnp.int32, (tk, tk), 0)
                    >= lax.broadcasted_iota(jnp.int32, (tk, tk), 1), 1.0, 0.0).astype(BF16)
    heads = range(N_HEADS)
    lanes = [slice(h * HEAD_DIM, (h + 1) * HEAD_DIM) for h in heads]

    def key_block(k_blk, v_blk, diagonal):
        z = [lax.dot_general(q_ref[:, lanes[h]], k_blk[:, lanes[h]],
                             (((1,), (1,)), ((), ())), preferred_element_type=F32)
             for h in heads]
        suffix = []
        for h in heads:
            softplus = jnp.maximum(z[h], 0.0) + jnp.log(1.0 + jnp.exp(-jnp.abs(z[h])))
            if diagonal:
                softplus = jnp.where(strictly_causal, softplus, 0.0)
            hi = softplus.astype(BF16)
            lo = (softplus - hi.astype(F32)).astype(BF16)
            both = jnp.dot(jnp.concatenate([hi, lo], axis=0), tri, preferred_element_type=F32)
            suffix.append(both[:tq] + both[tq:])
        for h in heads:
            carry = carry_s[h]
            attn = jnp.exp(z[h] - suffix[h] - carry)
            if diagonal:
                attn = jnp.where(strictly_causal, attn, 0.0)
            acc_s[:, lanes[h]] += jnp.dot(attn.astype(BF16), v_blk[:, lanes[h]],
                                          preferred_element_type=F32)
            carry_s[h] = carry + suffix[h][:, 0:1]

    def some_row_live():
        return jnp.min(carry_s[...]) < -LOG_F32_UNDERFLOW

    def diagonal_block():
        half = tq // 2
        parts = [(slice(0, half), half), (slice(half, tq), tk)]
        for rows, nk in parts:
            r0 = rows.start
            mask = (lax.broadcasted_iota(jnp.int32, (half, nk), 1)
                    < lax.broadcasted_iota(jnp.int32, (half, nk), 0) + r0)
            tri_p = tri[:nk, :nk]
            z = [lax.dot_general(q_ref[rows, lanes[h]], kd_ref[0:nk, lanes[h]],
                                 (((1,), (1,)), ((), ())), preferred_element_type=F32)
                 for h in heads]
            suffix = []
            for h in heads:
                softplus = jnp.maximum(z[h], 0.0) + jnp.log(1.0 + jnp.exp(-jnp.abs(z[h])))
                softplus = jnp.where(mask, softplus, 0.0)
                hi = softplus.astype(BF16)
                lo = (softplus - hi.astype(F32)).astype(BF16)
                both = jnp.dot(jnp.concatenate([hi, lo], axis=0), tri_p,
                               preferred_element_type=F32)
                suffix.append(both[:half] + both[half:])
            for h in heads:
                attn = jnp.where(mask, jnp.exp(z[h] - suffix[h]), 0.0)
                acc_s[rows, lanes[h]] = jnp.dot(attn.astype(BF16), vd_ref[0:nk, lanes[h]],
                                                preferred_element_type=F32)
                carry_s[h, rows, :] = suffix[h][:, 0:1]

    diagonal_block()

    @pl.when(i >= 1)
    def _():
        key_block(kp_ref, vp_ref, False)

    def cond(state):
        kb, live = state
        return jnp.logical_and(kb >= 0, live)

    def body(state):
        kb, _ = state
        start = pl.multiple_of(kb * tk, tk)
        copies = [pltpu.make_async_copy(src.at[pl.ds(start, tk), :], dst, sem.at[n])
                  for n, (src, dst) in enumerate(((k_hbm, kbuf_s), (v_hbm, vbuf_s)))]
        for copy in copies:
            copy.start()
        for copy in copies:
            copy.wait()
        key_block(kbuf_s, vbuf_s, False)
        return kb - 1, some_row_live()

    lax.while_loop(cond, body, (i - 2, some_row_live()))
    o_ref[...] = acc_s[...].astype(o_ref.dtype)


def _sb_attn(q, k, v):
    s, d = q.shape
    tq = Q_TILE
    assert Q_TILE == K_TILE
    blk = pl.BlockSpec((tq, d), lambda i: (i, 0))
    prev_blk = pl.BlockSpec((tq, d), lambda i: (jnp.maximum(i - 1, 0), 0))
    hbm = pl.BlockSpec(memory_space=pl.ANY)
    return pl.pallas_call(
        _sb_attn_kernel,
        grid=(s // tq,),
        in_specs=[blk, blk, blk, prev_blk, prev_blk, hbm, hbm],
        out_specs=blk,
        out_shape=jax.ShapeDtypeStruct((s, d), BF16),
        scratch_shapes=[
            pltpu.VMEM((tq, d), F32),
            pltpu.VMEM((N_HEADS, tq, 1), F32),
            pltpu.VMEM((K_TILE, d), BF16),
            pltpu.VMEM((K_TILE, d), BF16),
            pltpu.SemaphoreType.DMA((2,)),
        ],
        compiler_params=pltpu.CompilerParams(
            dimension_semantics=("arbitrary",), vmem_limit_bytes=VMEM_LIMIT),
        name="sb_attn",
    )(q, k, v, k, v, k, v)


def _mix_ffn_kernel(x_ref, attn_ref, yag_ref, gb_ref, mod_ref, gpost_ref, gpre_ref, gpost2_ref,
                    wb_ref, wo_ref, wua_ref, wub_ref, cw_a_ref, cw_b_ref, cb_a_ref, cb_b_ref,
                    wd_ref, o_ref, x1_s, h_s, acc_s, tail_a_s, tail_b_s):
    i = pl.program_id(0)
    c = pl.program_id(1)
    tm = x_ref.shape[0]
    sub = FFN_SUB_ROWS
    blocks = [slice(r * sub, (r + 1) * sub) for r in range(tm // sub)]

    @pl.when(c == 0)
    def _():
        y_b = [jnp.dot(attn_ref[rows, :], _weight(wb_ref), preferred_element_type=F32)
               for rows in blocks]
        y = []
        for r, rows in enumerate(blocks):
            merged = yag_ref[rows, :].astype(F32) + gb_ref[rows, :].astype(F32) * y_b[r]
            y.append(jnp.dot(merged.astype(BF16), _weight(wo_ref), preferred_element_type=F32))
        for r, rows in enumerate(blocks):
            x1 = x_ref[rows, :] + (1.0 + mod_ref[2:3, :]) * (_rms(y[r]) * gpost_ref[...])
            x1_s[rows, :] = x1
            hn = _rms(x1) * gpre_ref[...]
            h_s[rows, :] = (hn * (1.0 + mod_ref[4:5, :]) + mod_ref[3:4, :]).astype(BF16)
        acc_s[...] = jnp.zeros_like(acc_s)

    @pl.when(i == 0)
    def _():
        tail_a_s[c] = jnp.zeros(tail_a_s.shape[1:], F32)
        tail_b_s[c] = jnp.zeros(tail_b_s.shape[1:], F32)

    def up(rows):
        hr = h_s[rows, :]
        return (jnp.dot(hr, _weight(wua_ref), preferred_element_type=F32),
                jnp.dot(hr, _weight(wub_ref), preferred_element_type=F32))

    def conv(prev_tail, u, cw, cb):
        ext = jnp.concatenate([prev_tail, u], axis=0)
        out = cb
        for kk in range(CONV_WIDTH):
            shift = CONV_WIDTH - 1 - kk
            out = out + cw[kk:kk + 1, :] * ext[TAIL - shift:TAIL - shift + sub, :]
        return out

    cw_a, cb_a = cw_a_ref[...], cb_a_ref[...]
    cw_b, cb_b = 0.5 * cw_b_ref[...], 0.5 * cb_b_ref[...]

    def down(rows, tails, ua, ub):
        a = conv(tails[0], ua, cw_a, cb_a)
        b_half = conv(tails[1], ub, cw_b, cb_b)
        t = jnp.tanh(a * (GELU_C1 + GELU_C3 * (a * a)))
        act = (a + a * t) * b_half
        acc_s[rows, :] += jnp.dot(act.astype(BF16), _weight(wd_ref), preferred_element_type=F32)

    tails = (tail_a_s[c], tail_b_s[c])
    ups = up(blocks[0])
    for r, rows in enumerate(blocks):
        nxt = up(blocks[r + 1]) if r + 1 < len(blocks) else None
        down(rows, tails, *ups)
        tails = (ups[0][sub - TAIL:, :], ups[1][sub - TAIL:, :])
        ups = nxt
    tail_a_s[c] = tails[0]
    tail_b_s[c] = tails[1]

    @pl.when(c == pl.num_programs(1) - 1)
    def _():
        o_ref[...] = x1_s[...] + (1.0 + mod_ref[5:6, :]) * (_rms(acc_s[...]) * gpost2_ref[...])


def _mix_ffn(l, x, attn, yag, gb, mod, g_post, g_pre2, g_post2, w_b, w_o, w_up, conv_w, conv_b,
             w_down):
    s, d = x.shape
    tm, tc = FFN_ROW_TILE, FF_TILE
    nc = D_FF // tc
    row_spec = pl.BlockSpec((tm, d), lambda i, c: (i, 0))
    in_row_spec = pl.BlockSpec(
        (tm, d), lambda i, c: (jnp.where(c >= 1, jnp.minimum(i + 1, s // tm - 1), i), 0))
    vec_spec = pl.BlockSpec((None, 1, d), lambda i, c: (l, 0, 0))
    sq_spec = pl.BlockSpec((None, d // 2, d), lambda i, c: (l, 0, 0), pipeline_mode=pl.Buffered(1))
    return pl.pallas_call(
        _mix_ffn_kernel,
        grid=(s // tm, nc),
        in_specs=[
            in_row_spec, in_row_spec, in_row_spec, in_row_spec,
            pl.BlockSpec((None, N_MOD, d), lambda i, c: (l, 0, 0)),
            vec_spec, vec_spec, vec_spec,
            sq_spec, sq_spec,
            pl.BlockSpec((None, d // 2, tc), lambda i, c: (l, 0, c)),
            pl.BlockSpec((None, d // 2, tc), lambda i, c: (l, 0, nc + c)),
            pl.BlockSpec((None, CONV_WIDTH, tc), lambda i, c: (l, 0, c)),
            pl.BlockSpec((None, CONV_WIDTH, tc), lambda i, c: (l, 0, nc + c)),
            pl.BlockSpec((None, 1, tc), lambda i, c: (l, 0, c)),
            pl.BlockSpec((None, 1, tc), lambda i, c: (l, 0, nc + c)),
            pl.BlockSpec((None, tc // 2, d), lambda i, c: (l, c, 0)),
        ],
        out_specs=row_spec,
        out_shape=jax.ShapeDtypeStruct((s, d), F32),
        scratch_shapes=[
            pltpu.VMEM((tm, d), F32),
            pltpu.VMEM((tm, d), BF16),
            pltpu.VMEM((tm, d), F32),
            pltpu.VMEM((nc, TAIL, tc), F32),
            pltpu.VMEM((nc, TAIL, tc), F32),
        ],
        compiler_params=pltpu.CompilerParams(
            dimension_semantics=("arbitrary", "arbitrary"), vmem_limit_bytes=VMEM_LIMIT),
        name="mix_ffn",
    )(x, attn, yag, gb, mod, g_post, g_pre2, g_post2, w_b, w_o, w_up, w_up, conv_w, conv_w,
      conv_b, conv_b, w_down)


def kernel(x, c, w_ada, b_ada, g_pre_mix, g_post_mix, g_pre_ffn, g_post_ffn, w_in, g_sgu,
           w_spatial, b_spatial, w_a_out, w_b_out, w_o, w_up, conv_w, conv_b, w_down):
    batch, s, d = x.shape
    depth = w_in.shape[0]
    assert batch == 1 and d == D_MODEL
    assert s % MIX_ROW_TILE == 0 and s % FFN_ROW_TILE == 0 and s % Q_TILE == 0
    mod = _adaln_mod(c, w_ada, b_ada).reshape(depth, N_MOD, d)
    vec = lambda g: g.reshape(depth, 1, -1)
    w_in, w_a_out, w_b_out, w_o, w_up, w_down = (
        _pack_weight(w) for w in (w_in, w_a_out, w_b_out, w_o, w_up, w_down))
    b_sp_t = jnp.transpose(b_spatial, (0, 2, 1))
    xs = x.reshape(s, d)
    for l in range(depth):
        q, k, v, yag, gb = _mixer_in(l, xs, mod, vec(g_pre_mix), w_in, vec(g_sgu), w_spatial,
                                     b_sp_t, w_a_out)
        attn = _sb_attn(q, k, v)
        xs = _mix_ffn(l, xs, attn, yag, gb, mod, vec(g_post_mix), vec(g_pre_ffn),
                      vec(g_post_ffn), w_b_out, w_o, w_up, conv_w, vec(conv_b), w_down)
    return xs.reshape(batch, s, d)
```
